```python
import math
import jax, jax.numpy as jnp
from jax import lax
import numpy as np

D_MODEL = 2048
BATCH = 4
SEQ = 4096
DEPTH = 1

CHUNK = 64
SB_BLOCK = 128
SB_HEADS = 8
SB_HEAD_DIM = 128
GDN_HEADS = 8
GDN_DK = 128
GDN_DV = 128
CONV_W = 4
D_FF = 5504
EPS = 1e-6

D_SB = SB_HEADS * SB_HEAD_DIM
D_GDN_QK = GDN_HEADS * GDN_DK
D_GDN_V = GDN_HEADS * GDN_DV
D_GDN_CONV = 2 * D_GDN_QK + D_GDN_V
D_MIX = D_SB + D_GDN_V
D_IN = 3 * D_SB + D_GDN_CONV + 2 * GDN_HEADS + D_GDN_V

kernel_name = "hybrid_stickbreak_gdn_macaron"


def rms_norm(x, gain):
    x32 = x.astype(jnp.float32)
    y = x32 * lax.rsqrt(jnp.mean(x32 * x32, axis=-1, keepdims=True) + EPS)
    return (y * gain.astype(jnp.float32)).astype(x.dtype)


def l2_norm(x):
    x32 = x.astype(jnp.float32)
    return x32 * lax.rsqrt(jnp.sum(x32 * x32, axis=-1, keepdims=True) + EPS)


def swiglu(x, w_gate, w_up, w_down):
    return (jax.nn.silu(x @ w_gate) * (x @ w_up)) @ w_down


def stick_breaking_attention(q, k, v):
    bsz, nh, s_len, d = q.shape
    nblk = s_len // SB_BLOCK
    qb = q.reshape(bsz, nh, nblk, SB_BLOCK, d).transpose(2, 0, 1, 3, 4)
    key_pos = jnp.arange(s_len)
    scale = 1.0 / math.sqrt(d)

    def block(args):
        q_blk, i = args
        q_pos = i * SB_BLOCK + jnp.arange(SB_BLOCK)
        z = jnp.einsum('bhqd,bhkd->bhqk', q_blk, k,
                       preferred_element_type=jnp.float32) * scale
        mask = key_pos[None, :] < q_pos[:, None]
        log_beta = jax.nn.log_sigmoid(z)
        log_keep = jnp.where(mask, jax.nn.log_sigmoid(-z), 0.0)
        suffix = lax.cumsum(log_keep, axis=3, reverse=True) - log_keep
        w = jnp.where(mask, jnp.exp(log_beta + suffix), 0.0)
        return jnp.einsum('bhqk,bhkd->bhqd', w.astype(v.dtype), v)

    o = lax.map(block, (qb, jnp.arange(nblk)))
    return o.transpose(1, 2, 0, 3, 4).reshape(bsz, nh, s_len, d)


def causal_depthwise_conv(x, w):
    c = x.shape[-1]
    return lax.conv_general_dilated(
        x, w[:, None, :].astype(x.dtype), window_strides=(1,), padding=[(CONV_W - 1, 0)],
        dimension_numbers=('NWC', 'WIO', 'NWC'), feature_group_count=c)


def gated_delta_rule(q, k, v, g, beta):
    out_dtype = v.dtype
    bsz, nh, s_len, dk = q.shape
    dv = v.shape[-1]
    n = s_len // CHUNK
    f32 = jnp.float32
    q = q.astype(f32).reshape(bsz, nh, n, CHUNK, dk)
    k = k.astype(f32).reshape(bsz, nh, n, CHUNK, dk)
    v = v.astype(f32).reshape(bsz, nh, n, CHUNK, dv)
    beta = beta.astype(f32).reshape(bsz, nh, n, CHUNK)
    g = jnp.cumsum(g.astype(f32).reshape(bsz, nh, n, CHUNK), axis=-1)

    incl = jnp.tril(jnp.ones((CHUNK, CHUNK), dtype=bool))
    strict = jnp.tril(jnp.ones((CHUNK, CHUNK), dtype=bool), k=-1)
    decay = jnp.exp(jnp.where(incl, g[..., :, None] - g[..., None, :], -jnp.inf))

    kb = k * beta[..., None]
    a_strict = jnp.where(strict, jnp.einsum('bhncd,bhnsd->bhncs', kb, k) * decay, 0.0)
    t_lhs = a_strict + jnp.eye(CHUNK, dtype=f32)
    u = lax.linalg.triangular_solve(t_lhs, v * beta[..., None],
                                    left_side=True, lower=True, unit_diagonal=True)
    w = lax.linalg.triangular_solve(t_lhs, kb * jnp.exp(g)[..., None],
                                    left_side=True, lower=True, unit_diagonal=True)
    attn_qk = jnp.einsum('bhncd,bhnsd->bhncs', q, k) * decay
    g_last = g[..., -1]
    k_to_end = k * jnp.exp(g_last[..., None] - g)[..., None]
    q_dec = q * jnp.exp(g)[..., None]

    def mv(t):
        return jnp.moveaxis(t, 2, 0)

    def step(state, xs):
        u_c, w_c, qd_c, aqk_c, kend_c, gl_c = xs
        v_new = u_c - jnp.einsum('bhck,bhkv->bhcv', w_c, state)
        o_c = (jnp.einsum('bhck,bhkv->bhcv', qd_c, state)
               + jnp.einsum('bhcs,bhsv->bhcv', aqk_c, v_new))
        state = (state * jnp.exp(gl_c)[..., None, None]
                 + jnp.einsum('bhck,bhcv->bhkv', kend_c, v_new))
        return state, o_c

    state0 = jnp.zeros((bsz, nh, dk, dv), dtype=f32)
    _, o = lax.scan(step, state0, (mv(u), mv(w), mv(q_dec), mv(attn_qk), mv(k_to_end), mv(g_last)))
    o = jnp.moveaxis(o, 0, 2).reshape(bsz, nh, s_len, dv)
    return o.astype(out_dtype)


def hybrid_mixer(h, w_in, sb_out_norm, conv_w, a_log, dt_bias, gdn_out_norm, w_out):
    bsz, s_len, _ = h.shape
    proj = h @ w_in
    offs = np.cumsum([0, D_SB, D_SB, D_SB, D_GDN_CONV, GDN_HEADS, GDN_HEADS, D_GDN_V])
    q_sb, k_sb, v_sb, qkv_g, a_g, b_g, gate_g = [proj[..., offs[i]:offs[i + 1]] for i in range(7)]

    def heads(t, nh, d):
        return t.reshape(bsz, s_len, nh, d).transpose(0, 2, 1, 3)

    o_sb = stick_breaking_attention(heads(q_sb, SB_HEADS, SB_HEAD_DIM),
                                    heads(k_sb, SB_HEADS, SB_HEAD_DIM),
                                    heads(v_sb, SB_HEADS, SB_HEAD_DIM))
    o_sb = rms_norm(o_sb, sb_out_norm).transpose(0, 2, 1, 3).reshape(bsz, s_len, D_SB)

    qkv = jax.nn.silu(causal_depthwise_conv(qkv_g, conv_w))
    q_g = qkv[..., :D_GDN_QK]
    k_g = qkv[..., D_GDN_QK:2 * D_GDN_QK]
    v_g = qkv[..., 2 * D_GDN_QK:]
    q_g = l2_norm(heads(q_g, GDN_HEADS, GDN_DK)) * (1.0 / math.sqrt(GDN_DK))
    k_g = l2_norm(heads(k_g, GDN_HEADS, GDN_DK))
    v_g = heads(v_g, GDN_HEADS, GDN_DV)
    decay_log = -jnp.exp(a_log.astype(jnp.float32)) * jax.nn.softplus(
        a_g.astype(jnp.float32) + dt_bias.astype(jnp.float32))
    beta = jax.nn.sigmoid(b_g.astype(jnp.float32))
    o_g = gated_delta_rule(q_g, k_g, v_g, decay_log.transpose(0, 2, 1), beta.transpose(0, 2, 1))
    o_g = o_g.transpose(0, 2, 1, 3)
    gate = gate_g.reshape(bsz, s_len, GDN_HEADS, GDN_DV)
    o_g = (rms_norm(o_g, gdn_out_norm) * jax.nn.silu(gate)).reshape(bsz, s_len, D_GDN_V)

    y = jnp.concatenate([o_sb, o_g], axis=-1)
    return y @ w_out


def setup_inputs(seed: int = 0) -> dict:
    key = jax.random.key(seed)
    ks = jax.random.split(key, 20)
    f32 = jnp.float32

    def nrm(k, shape, fan_in):
        return jax.random.normal(k, shape, f32) * (fan_in ** -0.5)

    def gain(k, shape):
        return 1.0 + 0.02 * jax.random.normal(k, shape, f32)

    L = DEPTH
    x = jax.random.normal(ks[0], (BATCH, SEQ, D_MODEL), f32)
    dt = jnp.exp(jax.random.uniform(ks[9], (L, GDN_HEADS), f32, math.log(1e-3), math.log(1e-1)))
    dt_bias = dt + jnp.log(-jnp.expm1(-dt))
    a_log = jnp.log(jax.random.uniform(ks[8], (L, GDN_HEADS), f32, 1.0, 16.0))
    return {
        "x": x,
        "ffn1_norm": gain(ks[1], (L, D_MODEL)),
        "ffn1_w_gate": nrm(ks[2], (L, D_MODEL, D_FF), D_MODEL),
        "ffn1_w_up": nrm(ks[3], (L, D_MODEL, D_FF), D_MODEL),
        "ffn1_w_down": nrm(ks[4], (L, D_FF, D_MODEL), D_FF),
        "mix_norm": gain(ks[5], (L, D_MODEL)),
        "w_in": nrm(ks[6], (L, D_MODEL, D_IN), D_MODEL),
        "sb_out_norm": gain(ks[7], (L, SB_HEAD_DIM)),
        "conv_w": nrm(ks[10], (L, CONV_W, D_GDN_CONV), CONV_W),
        "a_log": a_log,
        "dt_bias": dt_bias,
        "gdn_out_norm": gain(ks[11], (L, GDN_DV)),
        "w_out": nrm(ks[12], (L, D_MIX, D_MODEL), D_MIX),
        "ffn2_norm": gain(ks[13], (L, D_MODEL)),
        "ffn2_w_gate": nrm(ks[14], (L, D_MODEL, D_FF), D_MODEL),
        "ffn2_w_up": nrm(ks[15], (L, D_MODEL, D_FF), D_MODEL),
        "ffn2_w_down": nrm(ks[16], (L, D_FF, D_MODEL), D_FF),
        "final_norm": gain(ks[17], (D_MODEL,)),
    }


def reference(x, ffn1_norm, ffn1_w_gate, ffn1_w_up, ffn1_w_down, mix_norm, w_in,
              sb_out_norm, conv_w, a_log, dt_bias, gdn_out_norm, w_out,
              ffn2_norm, ffn2_w_gate, ffn2_w_up, ffn2_w_down, final_norm):
    for l in range(DEPTH):
        x = x + 0.5 * swiglu(rms_norm(x, ffn1_norm[l]), ffn1_w_gate[l], ffn1_w_up[l], ffn1_w_down[l])
        h = rms_norm(x, mix_norm[l])
        x = x + hybrid_mixer(h, w_in[l], sb_out_norm[l], conv_w[l], a_log[l], dt_bias[l],
                             gdn_out_norm[l], w_out[l])
        x = x + 0.5 * swiglu(rms_norm(x, ffn2_norm[l]), ffn2_w_gate[l], ffn2_w_up[l], ffn2_w_down[l])
    return rms_norm(x, final_norm)
```

```python
import functools
import math

import jax
import jax.numpy as jnp
from jax import lax
from jax.experimental import pallas as pl
from jax.experimental.pallas import tpu as pltpu

F32 = jnp.float32
BF16 = jnp.bfloat16
EPS = 1e-6

SB_HEADS = 8
GDN_HEADS = 8
HEAD_DIM = 128
CHUNK = 64
CONV_W = 4
D_SB = SB_HEADS * HEAD_DIM
D_GDN = GDN_HEADS * HEAD_DIM

VMEM_LIMIT_BYTES = 56 * 1024 * 1024
HI = lax.Precision.HIGHEST


def _cparams(semantics):
    return pltpu.CompilerParams(dimension_semantics=semantics,
                                vmem_limit_bytes=VMEM_LIMIT_BYTES)


def _rms(x, gain):
    return x * lax.rsqrt(jnp.mean(x * x, axis=-1, keepdims=True) + EPS) * gain


def _sigmoid(x):
    return 1.0 / (1.0 + jnp.exp(-x))


def _silu(x):
    return x * _sigmoid(x)


def _dot(a, b, precision=None):
    return jnp.dot(a, b, preferred_element_type=F32, precision=precision)


def _dot_nt(a, b, precision=None):
    return lax.dot_general(a, b, (((1,), (1,)), ((), ())),
                           preferred_element_type=F32, precision=precision)


def _dot_tn(a, b, precision=None):
    return lax.dot_general(a, b, (((0,), (0,)), ((), ())),
                           preferred_element_type=F32, precision=precision)


def _ffn_kernel(x_ref, g_ref, wg_ref, wu_ref, wd_ref, g2_ref, *refs, n_f, post):
    if post == "mix":
        o_ref, h_ref, xn_ref, acc_ref = refs
    else:
        o_ref, xn_ref, acc_ref = refs
    f = pl.program_id(1)

    @pl.when(f == 0)
    def _():
        xn_ref[...] = _rms(x_ref[...], g_ref[...]).astype(BF16)
        acc_ref[...] = jnp.zeros_like(acc_ref)

    xn = xn_ref[...]
    gate = _dot(xn, wg_ref[...])
    up = _dot(xn, wu_ref[...])
    hid = (_silu(gate) * up).astype(BF16)
    acc_ref[...] += _dot(hid, wd_ref[...])

    @pl.when(f == n_f - 1)
    def _():
        y = x_ref[...] + 0.5 * acc_ref[...]
        if post == "mix":
            o_ref[...] = y
            h_ref[...] = _rms(y, g2_ref[...]).astype(BF16)
        else:
            o_ref[...] = _rms(y, g2_ref[...])


def _ffn(x, gain, wg, wu, wd, gain2, post, tm, tf):
    t, d = x.shape
    fp = wg.shape[1]
    n_f = fp // tf
    out_shape = [jax.ShapeDtypeStruct((t, d), F32)]
    out_specs = [pl.BlockSpec((tm, d), lambda i, f: (i, 0))]
    if post == "mix":
        out_shape.append(jax.ShapeDtypeStruct((t, d), BF16))
        out_specs.append(pl.BlockSpec((tm, d), lambda i, f: (i, 0)))
    res = pl.pallas_call(
        functools.partial(_ffn_kernel, n_f=n_f, post=post),
        grid=(t // tm, n_f),
        in_specs=[
            pl.BlockSpec((tm, d), lambda i, f: (i, 0)),
            pl.BlockSpec((1, d), lambda i, f: (0, 0)),
            pl.BlockSpec((d, tf), lambda i, f: (0, f)),
            pl.BlockSpec((d, tf), lambda i, f: (0, f)),
            pl.BlockSpec((tf, d), lambda i, f: (f, 0)),
            pl.BlockSpec((1, d), lambda i, f: (0, 0)),
        ],
        out_specs=out_specs,
        out_shape=out_shape,
        scratch_shapes=[pltpu.VMEM((tm, d), BF16), pltpu.VMEM((tm, d), F32)],
        compiler_params=_cparams(("parallel", "arbitrary")),
        name="ffn_" + post,
    )(x, gain, wg, wu, wd, gain2)
    return res


def _mm_kernel(x_ref, w_ref, o_ref):
    o_ref[...] = _dot(x_ref[...], w_ref[...]).astype(o_ref.dtype)


def _matmul(x, w, out_dtype, tm, tn):
    m, k = x.shape
    n = w.shape[1]
    return pl.pallas_call(
        _mm_kernel,
        grid=(m // tm, n // tn),
        in_specs=[pl.BlockSpec((tm, k), lambda i, j: (i, 0)),
                  pl.BlockSpec((k, tn), lambda i, j: (0, j))],
        out_specs=pl.BlockSpec((tm, tn), lambda i, j: (i, j)),
        out_shape=jax.ShapeDtypeStruct((m, n), out_dtype),
        compiler_params=_cparams(("parallel", "parallel")),
        name="in_proj_n%d" % n,
    )(x, w)


def _out_proj_kernel(ya_ref, yb_ref, wa_ref, wb_ref, r_ref, o_ref):
    o_ref[...] = (r_ref[...] + _dot(ya_ref[...], wa_ref[...])
                  + _dot(yb_ref[...], wb_ref[...]))


def _out_proj(ya, yb, w, resid, tm, tn):
    m, ka = ya.shape
    kb = yb.shape[1]
    n = w.shape[1]
    assert ka == kb
    return pl.pallas_call(
        _out_proj_kernel,
        grid=(m // tm, n // tn),
        in_specs=[pl.BlockSpec((tm, ka), lambda i, j: (i, 0)),
                  pl.BlockSpec((tm, kb), lambda i, j: (i, 0)),
                  pl.BlockSpec((ka, tn), lambda i, j: (0, j)),
                  pl.BlockSpec((kb, tn), lambda i, j: (1, j)),
                  pl.BlockSpec((tm, tn), lambda i, j: (i, j))],
        out_specs=pl.BlockSpec((tm, tn), lambda i, j: (i, j)),
        out_shape=jax.ShapeDtypeStruct((m, n), F32),
        compiler_params=_cparams(("parallel", "parallel")),
        name="out_proj",
    )(ya, yb, w, w, resid)


def _sb_kernel(q_ref, k_ref, v_ref, gain_ref, o_ref, *, tb, scale):
    i = pl.program_id(2)
    q = q_ref[0]
    row = lax.broadcasted_iota(jnp.int32, (tb, tb), 0)
    col = lax.broadcasted_iota(jnp.int32, (tb, tb), 1)
    later = (row > col).astype(BF16)

    def body(step, carry):
        acc, run = carry
        j = i - step
        start = pl.multiple_of(j * tb, tb)
        k = k_ref[0, pl.ds(start, tb), :]
        v = v_ref[0, pl.ds(start, tb), :]
        z = _dot_nt(q, k) * scale
        log_beta = jnp.minimum(z, 0.0) - jnp.log1p(jnp.exp(-jnp.abs(z)))
        mask = (col < row) | (step > 0)
        log_keep = jnp.where(mask, log_beta - z, 0.0)
        hi = log_keep.astype(BF16)
        lo = (log_keep - hi.astype(F32)).astype(BF16)
        suffix = _dot(hi, later) + _dot(lo, later)
        w = jnp.where(mask, jnp.exp(log_beta + suffix + run), 0.0)
        acc = acc + _dot(w.astype(BF16), v)
        run = run + jnp.sum(log_keep, axis=-1, keepdims=True)
        return acc, run

    acc0 = jnp.zeros((tb, HEAD_DIM), F32)
    run0 = jnp.zeros((tb, 1), F32)
    acc, _ = lax.fori_loop(0, i + 1, body, (acc0, run0))
    o_ref[0] = _rms(acc, gain_ref[...]).astype(o_ref.dtype)


def _sb_attention(proj_sb, gain, tb):
    b, s, _ = proj_sb.shape
    nh = SB_HEADS
    return pl.pallas_call(
        functools.partial(_sb_kernel, tb=tb, scale=1.0 / math.sqrt(HEAD_DIM)),
        grid=(b, nh, s // tb),
        in_specs=[
            pl.BlockSpec((1, tb, HEAD_DIM), lambda bi, h, i: (bi, i, h)),
            pl.BlockSpec((1, s, HEAD_DIM), lambda bi, h, i: (bi, 0, nh + h)),
            pl.BlockSpec((1, s, HEAD_DIM), lambda bi, h, i: (bi, 0, 2 * nh + h)),
            pl.BlockSpec((1, HEAD_DIM), lambda bi, h, i: (0, 0)),
        ],
        out_specs=pl.BlockSpec((1, tb, HEAD_DIM), lambda bi, h, i: (bi, i, h)),
        out_shape=jax.ShapeDtypeStruct((b, s, D_SB), BF16),
        compiler_params=_cparams(("parallel", "parallel", "arbitrary")),
        name="sb_attn",
    )(proj_sb, proj_sb, proj_sb, gain)


def _unit_lower_inverse(a, eye):
    r = lax.broadcasted_iota(jnp.int32, a.shape, 0)
    c = lax.broadcasted_iota(jnp.int32, a.shape, 1)
    same16 = (r // 16) == (c // 16)
    same32 = (r // 32) == (c // 32)
    a16 = jnp.where(same16, a, 0.0)
    x = eye - a16
    p = _dot(a16, a16, HI)
    x = x + _dot(x, p, HI)
    p = _dot(p, p, HI)
    x = x + _dot(x, p, HI)
    p = _dot(p, p, HI)
    x = x + _dot(x, p, HI)
    e32 = jnp.where(same32 & ~same16, a, 0.0)
    x = x - _dot(x, _dot(e32, x, HI), HI)
    e64 = jnp.where(same32, 0.0, a)
    x = x - _dot(x, _dot(e64, x, HI), HI)
    return x


def _gdn_kernel(alog_ref, dtb_ref, q_ref, k_ref, v_ref, gate_ref, a_ref, b_ref,
                cwq_ref, cwk_ref, cwv_ref, gain_ref, o_ref,
                ext_ref, state_ref, *, ts):
    h = pl.program_id(1)
    s = pl.program_id(2)
    nc = ts // CHUNK
    pad = 8

    @pl.when(s == 0)
    def _():
        ext_ref[:, 0:pad, :] = jnp.zeros((3, pad, HEAD_DIM), F32)
        state_ref[...] = jnp.zeros_like(state_ref)

    def conv_silu(idx, x_ref, cw_ref):
        ext_ref[idx, pad:pad + ts, :] = x_ref[0]
        acc = None
        for tap in range(CONV_W):
            off = pad - (CONV_W - 1) + tap
            term = cw_ref[tap:tap + 1, :] * ext_ref[idx, off:off + ts, :]
            acc = term if acc is None else acc + term
        ext_ref[idx, 0:pad, :] = ext_ref[idx, ts:ts + pad, :]
        return _silu(acc)

    def l2n(x):
        return x * lax.rsqrt(jnp.sum(x * x, axis=-1, keepdims=True) + EPS)

    q_all = l2n(conv_silu(0, q_ref, cwq_ref)) * (1.0 / math.sqrt(HEAD_DIM))
    k_all = l2n(conv_silu(1, k_ref, cwk_ref))
    v_all = conv_silu(2, v_ref, cwv_ref)

    a_rows = a_ref[0, 0]
    b_rows = b_ref[0, 0]
    neg_rate = -jnp.exp(jnp.full((1, CHUNK), alog_ref[h], F32))
    sp_in = a_rows + jnp.full((1, CHUNK), dtb_ref[h], F32)
    softplus = jnp.maximum(sp_in, 0.0) + jnp.log1p(jnp.exp(-jnp.abs(sp_in)))
    g_rows = neg_rate * softplus
    beta_rows = _sigmoid(b_rows)
    r = lax.broadcasted_iota(jnp.int32, (CHUNK, CHUNK), 0)
    c = lax.broadcasted_iota(jnp.int32, (CHUNK, CHUNK), 1)
    incl = r >= c
    strict = r > c
    eye = (r == c).astype(F32)
    gc_rows = _dot(g_rows, (r <= c).astype(F32), HI)
    gc_cols = _dot_nt(incl.astype(F32), g_rows, HI)
    beta_cols = _dot_nt(eye, beta_rows, HI)

    gain = gain_ref[...]
    for ci in range(nc):
        lo = ci * CHUNK
        qc = q_all[lo:lo + CHUNK]
        kc = k_all[lo:lo + CHUNK]
        vc = v_all[lo:lo + CHUNK]
        gc_col = gc_cols[:, ci:ci + 1]
        gc_row = gc_rows[ci:ci + 1, :]
        g_last = gc_row[:, CHUNK - 1:CHUNK]
        beta_col = beta_cols[:, ci:ci + 1]
        decay = jnp.where(incl, jnp.exp(jnp.where(incl, gc_col - gc_row, 0.0)), 0.0)
        kb = kc * beta_col
        a_mat = jnp.where(strict, _dot_nt(kb, kc, HI) * decay, 0.0)
        t_inv = _unit_lower_inverse(a_mat, eye)
        exp_g = jnp.exp(gc_col)
        rhs = jnp.concatenate([vc * beta_col, kb * exp_g], axis=1)
        uw = _dot(t_inv, rhs, HI)
        u = uw[:, :HEAD_DIM]
        w = uw[:, HEAD_DIM:]
        aqk = jnp.where(incl, _dot_nt(qc, kc, HI) * decay, 0.0)
        k_end = kc * jnp.exp(g_last - gc_col)
        q_dec = qc * exp_g

        state = state_ref[...]
        ws_qs = _dot(jnp.concatenate([w, q_dec], axis=0), state, HI)
        v_new = u - ws_qs[:CHUNK]
        o = ws_qs[CHUNK:] + _dot(aqk, v_new, HI)
        state_ref[...] = state * jnp.exp(g_last) + _dot_tn(k_end, v_new, HI)

        gate = gate_ref[0, lo:lo + CHUNK, :]
        o_ref[0, lo:lo + CHUNK, :] = (_rms(o, gain) * _silu(gate)).astype(o_ref.dtype)


def _gdn(proj_g, a_t, b_t, conv_w, a_log, dt_bias, gain, ts):
    b, s, _ = proj_g.shape
    nh = GDN_HEADS
    nc = ts // CHUNK
    smem = pl.BlockSpec(memory_space=pltpu.SMEM)

    def col(group):
        return pl.BlockSpec((1, ts, HEAD_DIM), lambda bi, h, si: (bi, si, group * nh + h))

    def cw(group):
        return pl.BlockSpec((CONV_W, HEAD_DIM), lambda bi, h, si: (0, group * nh + h))

    ab = pl.BlockSpec((1, 1, nc, CHUNK), lambda bi, h, si: (bi, h, si, 0))
    return pl.pallas_call(
        functools.partial(_gdn_kernel, ts=ts),
        grid=(b, nh, s // ts),
        in_specs=[smem, smem, col(0), col(1), col(2), col(3), ab, ab,
                  cw(0), cw(1), cw(2),
                  pl.BlockSpec((1, HEAD_DIM), lambda bi, h, si: (0, 0))],
        out_specs=pl.BlockSpec((1, ts, HEAD_DIM), lambda bi, h, si: (bi, si, h)),
        out_shape=jax.ShapeDtypeStruct((b, s, D_GDN), BF16),
        scratch_shapes=[pltpu.VMEM((3, ts + 8, HEAD_DIM), F32),
                        pltpu.VMEM((HEAD_DIM, HEAD_DIM), F32)],
        compiler_params=_cparams(("parallel", "parallel", "arbitrary")),
        name="gdn",
    )(a_log, dt_bias, proj_g, proj_g, proj_g, proj_g, a_t, b_t,
      conv_w, conv_w, conv_w, gain)


def _pick(n, pref):
    t = min(pref, n)
    while n % t:
        t -= 128
    return t


def _pad_ff(w, axis, mult):
    pad = (-w.shape[axis]) % mult
    if pad == 0:
        return w
    widths = [(0, 0)] * w.ndim
    widths[axis] = (0, pad)
    return jnp.pad(w, widths)


def _ffn_weights(wg, wu, wd, tf):
    return (_pad_ff(wg, 1, tf).astype(BF16), _pad_ff(wu, 1, tf).astype(BF16),
            _pad_ff(wd, 0, tf).astype(BF16))


def kernel(x, ffn1_norm, ffn1_w_gate, ffn1_w_up, ffn1_w_down, mix_norm, w_in, sb_out_norm, conv_w, a_log, dt_bias, gdn_out_norm, w_out, ffn2_norm, ffn2_w_gate, ffn2_w_up, ffn2_w_down, final_norm):
    bsz, s_len, d = x.shape
    t = bsz * s_len
    depth = ffn1_norm.shape[0]
    tm = _pick(t, 512)
    tf = 512
    x2d = x.reshape(t, d)
    row = lambda v: v.reshape(1, -1)

    for l in range(depth):
        wg, wu, wd = _ffn_weights(ffn1_w_gate[l], ffn1_w_up[l], ffn1_w_down[l], tf)
        x1, hmix = _ffn(x2d, row(ffn1_norm[l]), wg, wu, wd, row(mix_norm[l]), "mix", tm, tf)

        w = w_in[l]
        o_qkv = 3 * D_SB
        o_a = o_qkv + 3 * D_GDN
        o_gate = o_a + 2 * GDN_HEADS
        w_sb = w[:, :o_qkv].astype(BF16)
        w_g = jnp.concatenate([w[:, o_qkv:o_a], w[:, o_gate:]], axis=1).astype(BF16)
        w_ab = jnp.pad(w[:, o_a:o_gate], ((0, 0), (0, HEAD_DIM - 2 * GDN_HEADS))).astype(BF16)
        tmm = _pick(t, 1024)
        proj_sb = _matmul(hmix, w_sb, BF16, tmm, 1024).reshape(bsz, s_len, 3 * D_SB)
        proj_g = _matmul(hmix, w_g, F32, tmm, 1024).reshape(bsz, s_len, 4 * D_GDN)
        proj_ab = _matmul(hmix, w_ab, F32, tmm, HEAD_DIM).reshape(bsz, s_len, HEAD_DIM)

        def per_head(cols):
            return cols.transpose(0, 2, 1).reshape(bsz, GDN_HEADS, s_len // CHUNK, CHUNK)

        a_t = per_head(proj_ab[..., :GDN_HEADS])
        b_t = per_head(proj_ab[..., GDN_HEADS:2 * GDN_HEADS])

        o_sb = _sb_attention(proj_sb, row(sb_out_norm[l]), _pick(s_len, 256))
        o_g = _gdn(proj_g, a_t, b_t, conv_w[l], a_log[l], dt_bias[l],
                   row(gdn_out_norm[l]), _pick(s_len, 512))

        x2 = _out_proj(o_sb.reshape(t, D_SB), o_g.reshape(t, D_GDN),
                       w_out[l].astype(BF16), x1, tmm, _pick(d, 1024))

        wg, wu, wd = _ffn_weights(ffn2_w_gate[l], ffn2_w_up[l], ffn2_w_down[l], tf)
        if l == depth - 1:
            (x2d,) = _ffn(x2, row(ffn2_norm[l]), wg, wu, wd, row(final_norm), "final", tm, tf)
        else:
            raise NotImplementedError("depth > 1")
    return x2d.reshape(bsz, s_len, d)
```

```python
import functools
import math

import jax
import jax.numpy as jnp
from jax import lax
from jax.experimental import pallas as pl
from jax.experimental.pallas import tpu as pltpu

F32 = jnp.float32
BF16 = jnp.bfloat16
EPS = 1e-6

SB_HEADS = 8
GDN_HEADS = 8
HEAD_DIM = 128
CHUNK = 64
GROUP = 256
CONV_W = 4
D_SB = SB_HEADS * HEAD_DIM
D_GDN = GDN_HEADS * HEAD_DIM

VMEM_LIMIT_BYTES = 56 * 1024 * 1024


def _cparams(semantics):
    return pltpu.CompilerParams(dimension_semantics=semantics,
                                vmem_limit_bytes=VMEM_LIMIT_BYTES)


def _rms(x, gain):
    return x * lax.rsqrt(jnp.mean(x * x, axis=-1, keepdims=True) + EPS) * gain


def _sigmoid(x):
    return 1.0 / (1.0 + jnp.exp(-x))


def _silu(x):
    return x * _sigmoid(x)


def _dot(a, b):
    return jnp.dot(a, b, preferred_element_type=F32)


def _dot_nt(a, b):
    return lax.dot_general(a, b, (((1,), (1,)), ((), ())), preferred_element_type=F32)


def _split(x):
    hi = x.astype(BF16)
    lo = (x - hi.astype(F32)).astype(BF16)
    return hi, lo


def _ffn_kernel(x_ref, g_ref, wg_ref, wu_ref, wd_ref, g2_ref, *refs, n_f, post):
    if post == "mix":
        o_ref, h_ref, xn_ref, acc_ref = refs
    else:
        o_ref, xn_ref, acc_ref = refs
    f = pl.program_id(1)

    @pl.when(f == 0)
    def _():
        xn_ref[...] = _rms(x_ref[...], g_ref[...]).astype(BF16)
        acc_ref[...] = jnp.zeros_like(acc_ref)

    xn = xn_ref[...]
    gate = _dot(xn, wg_ref[...])
    up = _dot(xn, wu_ref[...])
    hid = (_silu(gate) * up).astype(BF16)
    acc_ref[...] += _dot(hid, wd_ref[...])

    @pl.when(f == n_f - 1)
    def _():
        y = x_ref[...] + 0.5 * acc_ref[...]
        if post == "mix":
            o_ref[...] = y
            h_ref[...] = _rms(y, g2_ref[...]).astype(BF16)
        else:
            o_ref[...] = _rms(y, g2_ref[...])


def _ffn(x, gain, wg, wu, wd, gain2, post, tm, tf):
    t, d = x.shape
    fp = wg.shape[1]
    n_f = fp // tf
    out_shape = [jax.ShapeDtypeStruct((t, d), F32)]
    out_specs = [pl.BlockSpec((tm, d), lambda i, f: (i, 0))]
    if post == "mix":
        out_shape.append(jax.ShapeDtypeStruct((t, d), BF16))
        out_specs.append(pl.BlockSpec((tm, d), lambda i, f: (i, 0)))
    res = pl.pallas_call(
        functools.partial(_ffn_kernel, n_f=n_f, post=post),
        grid=(t // tm, n_f),
        in_specs=[
            pl.BlockSpec((tm, d), lambda i, f: (i, 0)),
            pl.BlockSpec((1, d), lambda i, f: (0, 0)),
            pl.BlockSpec((d, tf), lambda i, f: (0, f)),
            pl.BlockSpec((d, tf), lambda i, f: (0, f)),
            pl.BlockSpec((tf, d), lambda i, f: (f, 0)),
            pl.BlockSpec((1, d), lambda i, f: (0, 0)),
        ],
        out_specs=out_specs,
        out_shape=out_shape,
        scratch_shapes=[pltpu.VMEM((tm, d), BF16), pltpu.VMEM((tm, d), F32)],
        compiler_params=_cparams(("parallel", "arbitrary")),
        name="ffn_" + post,
    )(x, gain, wg, wu, wd, gain2)
    return res


def _mm_kernel(x_ref, w_ref, o_ref):
    o_ref[...] = _dot(x_ref[...], w_ref[...]).astype(o_ref.dtype)


def _matmul(x, w, out_dtype, tm, tn):
    m, k = x.shape
    n = w.shape[1]
    return pl.pallas_call(
        _mm_kernel,
        grid=(m // tm, n // tn),
        in_specs=[pl.BlockSpec((tm, k), lambda i, j: (i, 0)),
                  pl.BlockSpec((k, tn), lambda i, j: (0, j))],
        out_specs=pl.BlockSpec((tm, tn), lambda i, j: (i, j)),
        out_shape=jax.ShapeDtypeStruct((m, n), out_dtype),
        compiler_params=_cparams(("parallel", "parallel")),
        name="in_proj_n%d" % n,
    )(x, w)


def _out_proj_kernel(ya_ref, yb_ref, wa_ref, wb_ref, r_ref, o_ref):
    o_ref[...] = (r_ref[...] + _dot(ya_ref[...], wa_ref[...])
                  + _dot(yb_ref[...], wb_ref[...]))


def _out_proj(ya, yb, w, resid, tm, tn):
    m, ka = ya.shape
    kb = yb.shape[1]
    n = w.shape[1]
    assert ka == kb
    return pl.pallas_call(
        _out_proj_kernel,
        grid=(m // tm, n // tn),
        in_specs=[pl.BlockSpec((tm, ka), lambda i, j: (i, 0)),
                  pl.BlockSpec((tm, kb), lambda i, j: (i, 0)),
                  pl.BlockSpec((ka, tn), lambda i, j: (0, j)),
                  pl.BlockSpec((kb, tn), lambda i, j: (1, j)),
                  pl.BlockSpec((tm, tn), lambda i, j: (i, j))],
        out_specs=pl.BlockSpec((tm, tn), lambda i, j: (i, j)),
        out_shape=jax.ShapeDtypeStruct((m, n), F32),
        compiler_params=_cparams(("parallel", "parallel")),
        name="out_proj",
    )(ya, yb, w, w, resid)


def _sb_kernel(q_ref, k_ref, v_ref, gain_ref, o_ref, acc_ref, run_ref, *, tq, tk, scale):
    i = pl.program_id(2)
    n_diag = tq // tk
    q = q_ref[0]
    row = lax.broadcasted_iota(jnp.int32, (tq, tk), 0)
    col = lax.broadcasted_iota(jnp.int32, (tq, tk), 1)
    r2 = lax.broadcasted_iota(jnp.int32, (tk, tk), 0)
    c2 = lax.broadcasted_iota(jnp.int32, (tk, tk), 1)
    later = (r2 > c2).astype(BF16)
    acc_ref[...] = jnp.zeros_like(acc_ref)
    run_ref[...] = jnp.zeros_like(run_ref)

    def block(j, mask):
        start = pl.multiple_of(j * tk, tk)
        k = k_ref[0, pl.ds(start, tk), :]
        v = v_ref[0, pl.ds(start, tk), :]
        z = _dot_nt(q, k) * scale
        log_beta = jnp.minimum(z, 0.0) - jnp.log(1.0 + jnp.exp(-jnp.abs(z)))
        log_keep = log_beta - z
        if mask is not None:
            log_keep = jnp.where(mask, log_keep, 0.0)
        hi, lo = _split(log_keep)
        suffix = _dot(hi, later) + _dot(lo, later)
        run = jnp.concatenate([run_ref[...]] * (tk // HEAD_DIM), axis=1)
        w = jnp.exp(log_beta + suffix + run)
        if mask is not None:
            w = jnp.where(mask, w, 0.0)
        acc_ref[...] += _dot(w.astype(BF16), v)
        run_ref[...] += jnp.broadcast_to(jnp.sum(log_keep, axis=-1, keepdims=True),
                                         run_ref.shape)

    for d in range(n_diag):
        off = (n_diag - 1 - d) * tk
        block(i * n_diag + (n_diag - 1 - d), (col + off) < row)

    def body(step, carry):
        block(i * n_diag - 1 - step, None)
        return carry

    lax.fori_loop(0, i * n_diag, body, 0)
    o_ref[0] = _rms(acc_ref[...], gain_ref[...]).astype(o_ref.dtype)


def _sb_attention(proj_sb, gain, tq, tk):
    b, s, _ = proj_sb.shape
    nh = SB_HEADS
    return pl.pallas_call(
        functools.partial(_sb_kernel, tq=tq, tk=tk, scale=1.0 / math.sqrt(HEAD_DIM)),
        grid=(b, nh, s // tq),
        in_specs=[
            pl.BlockSpec((1, tq, HEAD_DIM), lambda bi, h, i: (bi, i, h)),
            pl.BlockSpec((1, s, HEAD_DIM), lambda bi, h, i: (bi, 0, nh + h)),
            pl.BlockSpec((1, s, HEAD_DIM), lambda bi, h, i: (bi, 0, 2 * nh + h)),
            pl.BlockSpec((1, HEAD_DIM), lambda bi, h, i: (0, 0)),
        ],
        out_specs=pl.BlockSpec((1, tq, HEAD_DIM), lambda bi, h, i: (bi, i, h)),
        out_shape=jax.ShapeDtypeStruct((b, s, D_SB), BF16),
        scratch_shapes=[pltpu.VMEM((tq, HEAD_DIM), F32), pltpu.VMEM((tq, HEAD_DIM), F32)],
        compiler_params=_cparams(("parallel", "parallel", "arbitrary")),
        name="sb_attn",
    )(proj_sb, proj_sb, proj_sb, gain)


def _dot3(a_hi, a_lo, b_hi, b_lo):
    return _dot(a_hi, b_hi) + (_dot(a_lo, b_hi) + _dot(a_hi, b_lo))


def _unit_lower_inverse(a_bd, same_chunk):
    n = GROUP // CHUNK
    a_p = a_bd[0:CHUNK]
    for c in range(1, n):
        a_p = a_p + a_bd[c * CHUNK:(c + 1) * CHUNK]
    r = lax.broadcasted_iota(jnp.int32, (CHUNK, GROUP), 0)
    cl = lax.broadcasted_iota(jnp.int32, (CHUNK, GROUP), 1) & (CHUNK - 1)
    same16 = (r >> 4) == (cl >> 4)
    same32 = (r >> 5) == (cl >> 5)
    eye_p = (r == cl).astype(F32)

    def block_diag(x_p):
        return jnp.where(same_chunk, jnp.concatenate([x_p] * n, axis=0), 0.0)

    def mm(x_p, y_p):
        xh, xl = _split(x_p)
        yh, yl = _split(block_diag(y_p))
        return _dot3(xh, xl, yh, yl)

    a16 = jnp.where(same16, a_p, 0.0)
    x = eye_p - a16
    p = mm(a16, a16)
    x = x + mm(x, p)
    p = mm(p, p)
    x = x + mm(x, p)
    p = mm(p, p)
    x = x + mm(x, p)
    e32 = jnp.where(same32 & ~same16, a_p, 0.0)
    x = x - mm(x, mm(e32, x))
    e64 = jnp.where(same32, 0.0, a_p)
    x = x - mm(x, mm(e64, x))
    return block_diag(x)


def _gdn_group(q4, k4, v4, a_row, b_row, neg_rate, dt_bias, state):
    n = GROUP // CHUNK
    shape = (GROUP, GROUP)
    r = lax.broadcasted_iota(jnp.int32, shape, 0)
    c = lax.broadcasted_iota(jnp.int32, shape, 1)
    same_chunk = (r >> 6) == (c >> 6)
    lower_incl = same_chunk & (c <= r)
    strict = same_chunk & (c < r)
    eye = r == c

    sp_in = a_row + dt_bias
    g_row = neg_rate * (jnp.maximum(sp_in, 0.0) + jnp.log(1.0 + jnp.exp(-jnp.abs(sp_in))))
    beta_row = _sigmoid(b_row)
    g_b = jnp.broadcast_to(g_row, shape)

    def rowsum(x):
        return jnp.sum(x, axis=1, keepdims=True)

    def colsum(x):
        return jnp.sum(x, axis=0, keepdims=True)

    gc_col = rowsum(jnp.where(lower_incl, g_b, 0.0))
    glast_col = rowsum(jnp.where(same_chunk, g_b, 0.0))
    g_col = rowsum(jnp.where(eye, g_b, 0.0))
    beta_col = rowsum(jnp.where(eye, jnp.broadcast_to(beta_row, shape), 0.0))
    gcol_b = jnp.broadcast_to(g_col, shape)
    gc_row = colsum(jnp.where(same_chunk & (r <= c), gcol_b, 0.0))
    glast_row = colsum(jnp.where(same_chunk, gcol_b, 0.0))

    decay = jnp.where(lower_incl, jnp.exp(jnp.where(lower_incl, gc_col - gc_row, 0.0)), 0.0)
    exp_gc = jnp.exp(gc_col)
    kb4 = k4 * beta_col
    gram = _dot_nt(jnp.concatenate([kb4, q4], axis=0).astype(BF16), k4.astype(BF16))
    a_bd = jnp.where(strict, gram[:GROUP] * decay, 0.0)
    aqk_bd = (gram[GROUP:] * decay).astype(BF16)

    tinv = _unit_lower_inverse(a_bd, same_chunk)
    th, tl = _split(tinv)
    rh, rl = _split(jnp.concatenate([kb4 * exp_gc, v4 * beta_col], axis=1))
    wu = _dot3(th, tl, rh, rl)
    w4 = wu[:, :HEAD_DIM]
    u4 = wu[:, HEAD_DIM:]
    wu_b = wu.astype(BF16)

    kend_t = (k4 * jnp.exp(glast_col - gc_col)).T
    qdec4 = q4 * exp_gc
    lane_chunk = lax.broadcasted_iota(jnp.int32, kend_t.shape, 1) >> 6
    lane = lax.broadcasted_iota(jnp.int32, (1, GROUP), 1)

    states = []
    for ci in range(n):
        pn = _dot(jnp.where(lane_chunk == ci, kend_t, 0.0).astype(BF16), wu_b)
        dec = jnp.exp(jnp.sum(jnp.where(lane == ci * CHUNK, glast_row, 0.0), axis=1, keepdims=True))
        states.append(state)
        state = (dec * state + pn[:, HEAD_DIM:]) - _dot(pn[:, :HEAD_DIM].astype(BF16),
                                                         state.astype(BF16))

    v_new, o_q = [], []
    for ci in range(n):
        lo = ci * CHUNK
        lhs = jnp.concatenate([w4[lo:lo + CHUNK], qdec4[lo:lo + CHUNK]], axis=0).astype(BF16)
        res = _dot(lhs, states[ci].astype(BF16))
        v_new.append(u4[lo:lo + CHUNK] - res[:CHUNK])
        o_q.append(res[CHUNK:])
    v_new = jnp.concatenate(v_new, axis=0).astype(BF16)
    o4 = jnp.concatenate(o_q, axis=0) + _dot(aqk_bd, v_new)
    return o4, state


def _gdn_kernel(alog_ref, dtb_ref, q_ref, k_ref, v_ref, gate_ref, a_ref, b_ref,
                cwq_ref, cwk_ref, cwv_ref, gain_ref, o_ref,
                ext_ref, state_ref, *, ts):
    h = pl.program_id(1)
    s = pl.program_id(2)
    ng = ts // GROUP
    pad = 8

    @pl.when(s == 0)
    def _():
        ext_ref[:, 0:pad, :] = jnp.zeros((3, pad, HEAD_DIM), F32)
        state_ref[...] = jnp.zeros_like(state_ref)

    def conv_silu(idx, x_ref, cw_ref):
        ext_ref[idx, pad:pad + ts, :] = x_ref[0]
        acc = None
        for tap in range(CONV_W):
            off = pad - (CONV_W - 1) + tap
            term = cw_ref[tap:tap + 1, :] * ext_ref[idx, off:off + ts, :]
            acc = term if acc is None else acc + term
        ext_ref[idx, 0:pad, :] = ext_ref[idx, ts:ts + pad, :]
        return _silu(acc)

    def l2n(x):
        return x * lax.rsqrt(jnp.sum(x * x, axis=-1, keepdims=True) + EPS)

    q_all = l2n(conv_silu(0, q_ref, cwq_ref)) * (1.0 / math.sqrt(HEAD_DIM))
    k_all = l2n(conv_silu(1, k_ref, cwk_ref))
    v_all = conv_silu(2, v_ref, cwv_ref)

    neg_rate = -jnp.exp(jnp.full((1, GROUP), alog_ref[h], F32))
    dt_bias = jnp.full((1, GROUP), dtb_ref[h], F32)
    gain = gain_ref[...]
    state = state_ref[...]
    for gi in range(ng):
        lo = gi * GROUP
        a_row = a_ref[0, 0, pl.ds(s * ng + gi, 1), :]
        b_row = b_ref[0, 0, pl.ds(s * ng + gi, 1), :]
        o4, state = _gdn_group(q_all[lo:lo + GROUP], k_all[lo:lo + GROUP], v_all[lo:lo + GROUP],
                               a_row, b_row, neg_rate, dt_bias, state)
        gate = gate_ref[0, lo:lo + GROUP, :]
        o_ref[0, lo:lo + GROUP, :] = (_rms(o4, gain) * _silu(gate)).astype(o_ref.dtype)
    state_ref[...] = state


def _gdn(proj_g, a_t, b_t, conv_w, a_log, dt_bias, gain, ts):
    b, s, _ = proj_g.shape
    nh = GDN_HEADS
    smem = pl.BlockSpec(memory_space=pltpu.SMEM)

    def col(group):
        return pl.BlockSpec((1, ts, HEAD_DIM), lambda bi, h, si: (bi, si, group * nh + h))

    def cw(group):
        return pl.BlockSpec((CONV_W, HEAD_DIM), lambda bi, h, si: (0, group * nh + h))

    ab = pl.BlockSpec((1, 1, s // GROUP, GROUP), lambda bi, h, si: (bi, h, 0, 0))
    return pl.pallas_call(
        functools.partial(_gdn_kernel, ts=ts),
        grid=(b, nh, s // ts),
        in_specs=[smem, smem, col(0), col(1), col(2), col(3), ab, ab,
                  cw(0), cw(1), cw(2),
                  pl.BlockSpec((1, HEAD_DIM), lambda bi, h, si: (0, 0))],
        out_specs=pl.BlockSpec((1, ts, HEAD_DIM), lambda bi, h, si: (bi, si, h)),
        out_shape=jax.ShapeDtypeStruct((b, s, D_GDN), BF16),
        scratch_shapes=[pltpu.VMEM((3, ts + 8, HEAD_DIM), F32),
                        pltpu.VMEM((HEAD_DIM, HEAD_DIM), F32)],
        compiler_params=_cparams(("parallel", "parallel", "arbitrary")),
        name="gdn",
    )(a_log, dt_bias, proj_g, proj_g, proj_g, proj_g, a_t, b_t,
      conv_w, conv_w, conv_w, gain)


def _pick(n, pref):
    t = min(pref, n)
    while n % t:
        t -= 128
    return t


def _pad_ff(w, axis, mult):
    pad = (-w.shape[axis]) % mult
    if pad == 0:
        return w
    widths = [(0, 0)] * w.ndim
    widths[axis] = (0, pad)
    return jnp.pad(w, widths)


def _ffn_weights(wg, wu, wd, tf):
    return (_pad_ff(wg, 1, tf).astype(BF16), _pad_ff(wu, 1, tf).astype(BF16),
            _pad_ff(wd, 0, tf).astype(BF16))


def kernel(x, ffn1_norm, ffn1_w_gate, ffn1_w_up, ffn1_w_down, mix_norm, w_in, sb_out_norm, conv_w, a_log, dt_bias, gdn_out_norm, w_out, ffn2_norm, ffn2_w_gate, ffn2_w_up, ffn2_w_down, final_norm):
    bsz, s_len, d = x.shape
    t = bsz * s_len
    assert ffn1_norm.shape[0] == 1, "single-layer block"
    l = 0
    tm = _pick(t, 512)
    tf = 512
    x2d = x.reshape(t, d)
    row = lambda v: v.reshape(1, -1)

    wg, wu, wd = _ffn_weights(ffn1_w_gate[l], ffn1_w_up[l], ffn1_w_down[l], tf)
    x1, hmix = _ffn(x2d, row(ffn1_norm[l]), wg, wu, wd, row(mix_norm[l]), "mix", tm, tf)

    w = w_in[l]
    o_qkv = 3 * D_SB
    o_a = o_qkv + 3 * D_GDN
    o_gate = o_a + 2 * GDN_HEADS
    w_sb = w[:, :o_qkv].astype(BF16)
    w_g = jnp.concatenate([w[:, o_qkv:o_a], w[:, o_gate:]], axis=1).astype(BF16)
    w_ab = jnp.pad(w[:, o_a:o_gate], ((0, 0), (0, HEAD_DIM - 2 * GDN_HEADS))).astype(BF16)
    tmm = _pick(t, 1024)
    proj_sb = _matmul(hmix, w_sb, BF16, tmm, 1024).reshape(bsz, s_len, 3 * D_SB)
    proj_g = _matmul(hmix, w_g, F32, tmm, 1024).reshape(bsz, s_len, 4 * D_GDN)
    proj_ab = _matmul(hmix, w_ab, F32, tmm, HEAD_DIM).reshape(bsz, s_len, HEAD_DIM)

    def per_head(cols):
        return cols.transpose(0, 2, 1).reshape(bsz, GDN_HEADS, s_len // GROUP, GROUP)

    a_t = per_head(proj_ab[..., :GDN_HEADS])
    b_t = per_head(proj_ab[..., GDN_HEADS:2 * GDN_HEADS])

    tq = _pick(s_len, 512)
    o_sb = _sb_attention(proj_sb, row(sb_out_norm[l]), tq, _pick(tq, 256))
    o_g = _gdn(proj_g, a_t, b_t, conv_w[l], a_log[l], dt_bias[l],
               row(gdn_out_norm[l]), _pick(s_len, 512))

    x2 = _out_proj(o_sb.reshape(t, D_SB), o_g.reshape(t, D_GDN),
                   w_out[l].astype(BF16), x1, tmm, _pick(d, 1024))

    wg, wu, wd = _ffn_weights(ffn2_w_gate[l], ffn2_w_up[l], ffn2_w_down[l], tf)
    (out,) = _ffn(x2, row(ffn2_norm[l]), wg, wu, wd, row(final_norm), "final", tm, tf)
    return out.reshape(bsz, s_len, d)
```

```python
import functools
import math

import jax
import jax.numpy as jnp
from jax import lax
from jax.experimental import pallas as pl
from jax.experimental.pallas import tpu as pltpu

F32 = jnp.float32
BF16 = jnp.bfloat16
EPS = 1e-6

SB_HEADS = 8
GDN_HEADS = 8
HEAD_DIM = 128
CHUNK = 64
GROUP = 256
CONV_W = 4
D_SB = SB_HEADS * HEAD_DIM
D_GDN = GDN_HEADS * HEAD_DIM

VMEM_LIMIT_BYTES = 56 * 1024 * 1024
MASKED_LOG = -1e30


def _cparams(semantics):
    return pltpu.CompilerParams(dimension_semantics=semantics,
                                vmem_limit_bytes=VMEM_LIMIT_BYTES)


def _rms(x, gain):
    return x * lax.rsqrt(jnp.mean(x * x, axis=-1, keepdims=True) + EPS) * gain


def _sigmoid(x):
    return 1.0 / (1.0 + jnp.exp(-x))


def _silu(x):
    return x * _sigmoid(x)


def _dot(a, b):
    return jnp.dot(a, b, preferred_element_type=F32)


def _dot_nt(a, b):
    return lax.dot_general(a, b, (((1,), (1,)), ((), ())), preferred_element_type=F32)


def _split(x):
    hi = x.astype(BF16)
    lo = (x - hi.astype(F32)).astype(BF16)
    return hi, lo


def _ffn_kernel(x_ref, g_ref, wg_ref, wu_ref, wd_ref, g2_ref, *refs, n_f, post):
    if post == "mix":
        o_ref, h_ref, xn_ref, acc_ref = refs
    else:
        o_ref, xn_ref, acc_ref = refs
    f = pl.program_id(1)

    @pl.when(f == 0)
    def _():
        xn_ref[...] = _rms(x_ref[...], g_ref[...]).astype(BF16)
        acc_ref[...] = jnp.zeros_like(acc_ref)

    xn = xn_ref[...]
    gate = _dot(xn, wg_ref[...])
    up = _dot(xn, wu_ref[...])
    hid = (_silu(gate) * up).astype(BF16)
    acc_ref[...] += _dot(hid, wd_ref[...])

    @pl.when(f == n_f - 1)
    def _():
        y = x_ref[...] + 0.5 * acc_ref[...]
        if post == "mix":
            o_ref[...] = y
            h_ref[...] = _rms(y, g2_ref[...]).astype(BF16)
        else:
            o_ref[...] = _rms(y, g2_ref[...])


def _ffn(x, gain, wg, wu, wd, gain2, post, tm, tf):
    t, d = x.shape
    fp = wg.shape[1]
    n_f = fp // tf
    out_shape = [jax.ShapeDtypeStruct((t, d), F32)]
    out_specs = [pl.BlockSpec((tm, d), lambda i, f: (i, 0))]
    if post == "mix":
        out_shape.append(jax.ShapeDtypeStruct((t, d), BF16))
        out_specs.append(pl.BlockSpec((tm, d), lambda i, f: (i, 0)))
    res = pl.pallas_call(
        functools.partial(_ffn_kernel, n_f=n_f, post=post),
        grid=(t // tm, n_f),
        in_specs=[
            pl.BlockSpec((tm, d), lambda i, f: (i, 0)),
            pl.BlockSpec((1, d), lambda i, f: (0, 0)),
            pl.BlockSpec((d, tf), lambda i, f: (0, f)),
            pl.BlockSpec((d, tf), lambda i, f: (0, f)),
            pl.BlockSpec((tf, d), lambda i, f: (f, 0)),
            pl.BlockSpec((1, d), lambda i, f: (0, 0)),
        ],
        out_specs=out_specs,
        out_shape=out_shape,
        scratch_shapes=[pltpu.VMEM((tm, d), BF16), pltpu.VMEM((tm, d), F32)],
        compiler_params=_cparams(("parallel", "arbitrary")),
        name="ffn_" + post,
    )(x, gain, wg, wu, wd, gain2)
    return res


def _mm_kernel(x_ref, w_ref, o_ref):
    o_ref[...] = _dot(x_ref[...], w_ref[...]).astype(o_ref.dtype)


def _matmul(x, w, out_dtype, tm, tn):
    m, k = x.shape
    n = w.shape[1]
    return pl.pallas_call(
        _mm_kernel,
        grid=(m // tm, n // tn),
        in_specs=[pl.BlockSpec((tm, k), lambda i, j: (i, 0)),
                  pl.BlockSpec((k, tn), lambda i, j: (0, j))],
        out_specs=pl.BlockSpec((tm, tn), lambda i, j: (i, j)),
        out_shape=jax.ShapeDtypeStruct((m, n), out_dtype),
        compiler_params=_cparams(("parallel", "parallel")),
        name="in_proj_n%d" % n,
    )(x, w)


def _out_proj_kernel(ya_ref, yb_ref, wa_ref, wb_ref, r_ref, o_ref):
    o_ref[...] = (r_ref[...] + _dot(ya_ref[...], wa_ref[...])
                  + _dot(yb_ref[...], wb_ref[...]))


def _out_proj(ya, yb, w, resid, tm, tn):
    m, ka = ya.shape
    kb = yb.shape[1]
    n = w.shape[1]
    assert ka == kb
    return pl.pallas_call(
        _out_proj_kernel,
        grid=(m // tm, n // tn),
        in_specs=[pl.BlockSpec((tm, ka), lambda i, j: (i, 0)),
                  pl.BlockSpec((tm, kb), lambda i, j: (i, 0)),
                  pl.BlockSpec((ka, tn), lambda i, j: (0, j)),
                  pl.BlockSpec((kb, tn), lambda i, j: (1, j)),
                  pl.BlockSpec((tm, tn), lambda i, j: (i, j))],
        out_specs=pl.BlockSpec((tm, tn), lambda i, j: (i, j)),
        out_shape=jax.ShapeDtypeStruct((m, n), F32),
        compiler_params=_cparams(("parallel", "parallel")),
        name="out_proj",
    )(ya, yb, w, w, resid)


def _sb_kernel(q_ref, k_ref, v_ref, gain_ref, o_ref,
               acc_ref, run_ref, lb_ref, hi_ref, lo_ref, rs_ref, *, tq, tk, scale):
    assert tq == 2 * tk
    i = pl.program_id(2)
    sub = HEAD_DIM
    nsub = tq // sub
    row = lax.broadcasted_iota(jnp.int32, (sub, tk), 0)
    col = lax.broadcasted_iota(jnp.int32, (sub, tk), 1)
    r2 = lax.broadcasted_iota(jnp.int32, (tk, tk), 0)
    c2 = lax.broadcasted_iota(jnp.int32, (tk, tk), 1)
    later = (r2 > c2).astype(BF16)
    acc_ref[...] = jnp.zeros_like(acc_ref)
    run_ref[...] = jnp.zeros_like(run_ref)

    def rows(r):
        return slice(r * sub, (r + 1) * sub)

    def mask_kind(diag_off, r):
        if diag_off is None or diag_off + tk - 1 < r * sub:
            return "all"
        if diag_off >= (r + 1) * sub - 1:
            return "none"
        return "part"

    def step(weigh_blk, score_blk):
        zs, sfx = {}, {}
        if score_blk is not None:
            js, slot_s, off_s = score_blk
            k = k_ref[0, pl.ds(pl.multiple_of(js * tk, tk), tk), :]
            for r in range(nsub):
                if mask_kind(off_s, r) != "none":
                    zs[r] = _dot_nt(q_ref[0, rows(r), :], k)
        if weigh_blk is not None:
            jw, slot_w, off_w = weigh_blk
            v = v_ref[0, pl.ds(pl.multiple_of(jw * tk, tk), tk), :]
            for r in range(nsub):
                if mask_kind(off_w, r) != "none":
                    sfx[r] = (_dot(hi_ref[slot_w, rows(r), :], later)
                              + _dot(lo_ref[slot_w, rows(r), :], later))
        for r in range(nsub):
            if r in zs:
                z = zs[r] * scale
                log_beta = jnp.minimum(z, 0.0) - jnp.log(1.0 + jnp.exp(-jnp.abs(z)))
                log_keep = log_beta - z
                if mask_kind(off_s, r) == "part":
                    mask = (col + off_s) < (row + r * sub)
                    log_keep = jnp.where(mask, log_keep, 0.0)
                    log_beta = jnp.where(mask, log_beta, MASKED_LOG)
                hi, lo = _split(log_keep)
                lb_ref[slot_s, rows(r), :] = log_beta
                hi_ref[slot_s, rows(r), :] = hi
                lo_ref[slot_s, rows(r), :] = lo
                rs_ref[slot_s, rows(r), :] = jnp.broadcast_to(
                    jnp.sum(log_keep, axis=-1, keepdims=True), (sub, HEAD_DIM))
            if r in sfx:
                run = jnp.concatenate([run_ref[rows(r), :]] * (tk // HEAD_DIM), axis=1)
                w = jnp.exp(lb_ref[slot_w, rows(r), :] + sfx[r] + run)
                acc_ref[rows(r), :] += _dot(w.astype(BF16), v)
                run_ref[rows(r), :] += rs_ref[slot_w, rows(r), :]

    step(None, (2 * i + 1, 0, tk))
    step((2 * i + 1, 0, tk), (2 * i, 1, 0))

    def body(p, carry):
        j = 2 * (i - p)
        step((j, 1, None), (j - 1, 0, None))
        step((j - 1, 0, None), (j - 2, 1, None))
        return carry

    lax.fori_loop(0, i, body, 0)
    step((0, 1, None), None)
    o_ref[0] = _rms(acc_ref[...], gain_ref[...]).astype(o_ref.dtype)


def _sb_attention(proj_sb, gain, tq, tk):
    b, s, _ = proj_sb.shape
    nh = SB_HEADS
    return pl.pallas_call(
        functools.partial(_sb_kernel, tq=tq, tk=tk, scale=1.0 / math.sqrt(HEAD_DIM)),
        grid=(b, nh, s // tq),
        in_specs=[
            pl.BlockSpec((1, tq, HEAD_DIM), lambda bi, h, i: (bi, i, h)),
            pl.BlockSpec((1, s, HEAD_DIM), lambda bi, h, i: (bi, 0, nh + h)),
            pl.BlockSpec((1, s, HEAD_DIM), lambda bi, h, i: (bi, 0, 2 * nh + h)),
            pl.BlockSpec((1, HEAD_DIM), lambda bi, h, i: (0, 0)),
        ],
        out_specs=pl.BlockSpec((1, tq, HEAD_DIM), lambda bi, h, i: (bi, i, h)),
        out_shape=jax.ShapeDtypeStruct((b, s, D_SB), BF16),
        scratch_shapes=[pltpu.VMEM((tq, HEAD_DIM), F32), pltpu.VMEM((tq, HEAD_DIM), F32),
                        pltpu.VMEM((2, tq, tk), F32), pltpu.VMEM((2, tq, tk), BF16),
                        pltpu.VMEM((2, tq, tk), BF16), pltpu.VMEM((2, tq, HEAD_DIM), F32)],
        compiler_params=_cparams(("parallel", "parallel", "arbitrary")),
        name="sb_attn",
    )(proj_sb, proj_sb, proj_sb, gain)


def _dot3(a_hi, a_lo, b_hi, b_lo):
    return _dot(a_hi, b_hi) + (_dot(a_lo, b_hi) + _dot(a_hi, b_lo))


def _unit_lower_inverse(a_bds, same_chunk):
    n = GROUP // CHUNK
    r = lax.broadcasted_iota(jnp.int32, (CHUNK, GROUP), 0)
    cl = lax.broadcasted_iota(jnp.int32, (CHUNK, GROUP), 1) & (CHUNK - 1)
    same16 = (r >> 4) == (cl >> 4)
    same32 = (r >> 5) == (cl >> 5)
    eye_p = (r == cl).astype(F32)

    def packed(a_bd):
        a_p = a_bd[0:CHUNK]
        for c in range(1, n):
            a_p = a_p + a_bd[c * CHUNK:(c + 1) * CHUNK]
        return a_p

    def block_diag(x_p):
        return jnp.where(same_chunk, jnp.concatenate([x_p] * n, axis=0), jnp.zeros((), x_p.dtype))

    def mm(xs, ys):
        out = []
        for x_p, y_p in zip(xs, ys):
            xh, xl = _split(x_p)
            yh, yl = _split(y_p)
            out.append(_dot3(xh, xl, block_diag(yh), block_diag(yl)))
        return out

    def axpy(xs, ys, sign):
        return [x + y if sign > 0 else x - y for x, y in zip(xs, ys)]

    a_ps = [packed(a) for a in a_bds]
    a16 = [jnp.where(same16, a, 0.0) for a in a_ps]
    x = [eye_p - a for a in a16]
    p = mm(a16, a16)
    x = axpy(x, mm(x, p), 1.0)
    p = mm(p, p)
    x = axpy(x, mm(x, p), 1.0)
    p = mm(p, p)
    x = axpy(x, mm(x, p), 1.0)
    e32 = [jnp.where(same32 & ~same16, a, 0.0) for a in a_ps]
    x = axpy(x, mm(x, mm(e32, x)), -1.0)
    e64 = [jnp.where(same32, 0.0, a) for a in a_ps]
    x = axpy(x, mm(x, mm(e64, x)), -1.0)
    return [block_diag(xi) for xi in x]


def _gdn_groups(qs, ks, vs, a_rows, b_rows, neg_rate, dt_bias, state):
    n = GROUP // CHUNK
    ng = len(qs)
    shape = (GROUP, GROUP)
    r = lax.broadcasted_iota(jnp.int32, shape, 0)
    c = lax.broadcasted_iota(jnp.int32, shape, 1)
    same_chunk = (r >> 6) == (c >> 6)
    lower_incl = same_chunk & (c <= r)
    upper_incl = same_chunk & (r <= c)
    strict = same_chunk & (c < r)
    eye = r == c

    def rowsum(x):
        return jnp.sum(x, axis=1, keepdims=True)

    def colsum(x):
        return jnp.sum(x, axis=0, keepdims=True)

    decay, exp_gc, kend_scale, glast_rows, beta_cols = [], [], [], [], []
    for a_row, b_row in zip(a_rows, b_rows):
        sp_in = a_row + dt_bias
        g_row = neg_rate * (jnp.maximum(sp_in, 0.0) + jnp.log(1.0 + jnp.exp(-jnp.abs(sp_in))))
        g_b = jnp.broadcast_to(g_row, shape)
        gc_col = rowsum(jnp.where(lower_incl, g_b, 0.0))
        glast_col = rowsum(jnp.where(same_chunk, g_b, 0.0))
        g_col = rowsum(jnp.where(eye, g_b, 0.0))
        beta_cols.append(rowsum(jnp.where(eye, jnp.broadcast_to(_sigmoid(b_row), shape), 0.0)))
        gcol_b = jnp.broadcast_to(g_col, shape)
        gc_row = colsum(jnp.where(upper_incl, gcol_b, 0.0))
        glast_rows.append(colsum(jnp.where(same_chunk, gcol_b, 0.0)))
        decay.append(jnp.where(lower_incl,
                               jnp.exp(jnp.where(lower_incl, gc_col - gc_row, 0.0)), 0.0))
        exp_gc.append(jnp.exp(gc_col))
        kend_scale.append(jnp.exp(glast_col - gc_col))

    kbs = [k * b for k, b in zip(ks, beta_cols)]
    grams = [_dot_nt(jnp.concatenate([kb, q], axis=0).astype(BF16), k.astype(BF16))
             for kb, q, k in zip(kbs, qs, ks)]
    a_bds = [jnp.where(strict, g[:GROUP] * d, 0.0) for g, d in zip(grams, decay)]
    aqk_bds = [(g[GROUP:] * d).astype(BF16) for g, d in zip(grams, decay)]

    tinvs = _unit_lower_inverse(a_bds, same_chunk)
    wus = []
    for tinv, kb, v, eg, bc in zip(tinvs, kbs, vs, exp_gc, beta_cols):
        th, tl = _split(tinv)
        rh, rl = _split(jnp.concatenate([kb * eg, v * bc], axis=1))
        wus.append(_dot3(th, tl, rh, rl))

    lane_chunk = lax.broadcasted_iota(jnp.int32, (HEAD_DIM, GROUP), 1) >> 6
    lane = lax.broadcasted_iota(jnp.int32, (1, GROUP), 1)
    pns, decs = [], []
    for k, scale, wu, gl in zip(ks, kend_scale, wus, glast_rows):
        kend_t = (k * scale).T
        wu_b = wu.astype(BF16)
        for ci in range(n):
            pns.append(_dot(jnp.where(lane_chunk == ci, kend_t, 0.0).astype(BF16), wu_b))
            decs.append(jnp.exp(jnp.sum(jnp.where(lane == ci * CHUNK, gl, 0.0),
                                        axis=1, keepdims=True)))

    states = []
    for pn, dec in zip(pns, decs):
        states.append(state.astype(BF16))
        state = (dec * state + pn[:, HEAD_DIM:]) - _dot(pn[:, :HEAD_DIM].astype(BF16), states[-1])

    outs = []
    for gi in range(ng):
        w4 = wus[gi][:, :HEAD_DIM]
        u4 = wus[gi][:, HEAD_DIM:]
        qdec4 = qs[gi] * exp_gc[gi]
        v_new, o_q = [], []
        for ci in range(n):
            lo = ci * CHUNK
            lhs = jnp.concatenate([w4[lo:lo + CHUNK], qdec4[lo:lo + CHUNK]], axis=0).astype(BF16)
            res = _dot(lhs, states[gi * n + ci])
            v_new.append(u4[lo:lo + CHUNK] - res[:CHUNK])
            o_q.append(res[CHUNK:])
        v_new = jnp.concatenate(v_new, axis=0).astype(BF16)
        outs.append(jnp.concatenate(o_q, axis=0) + _dot(aqk_bds[gi], v_new))
    return outs, state


def _gdn_kernel(alog_ref, dtb_ref, q_ref, k_ref, v_ref, gate_ref, a_ref, b_ref,
                cwq_ref, cwk_ref, cwv_ref, gain_ref, o_ref,
                ext_ref, state_ref, *, ts):
    h = pl.program_id(1)
    s = pl.program_id(2)
    ng = ts // GROUP
    pad = 8

    @pl.when(s == 0)
    def _():
        ext_ref[:, 0:pad, :] = jnp.zeros((3, pad, HEAD_DIM), F32)
        state_ref[...] = jnp.zeros_like(state_ref)

    def conv_silu(idx, x_ref, cw_ref):
        ext_ref[idx, pad:pad + ts, :] = x_ref[0]
        acc = None
        for tap in range(CONV_W):
            off = pad - (CONV_W - 1) + tap
            term = cw_ref[tap:tap + 1, :] * ext_ref[idx, off:off + ts, :]
            acc = term if acc is None else acc + term
        ext_ref[idx, 0:pad, :] = ext_ref[idx, ts:ts + pad, :]
        return _silu(acc)

    def l2n(x):
        return x * lax.rsqrt(jnp.sum(x * x, axis=-1, keepdims=True) + EPS)

    q_all = l2n(conv_silu(0, q_ref, cwq_ref)) * (1.0 / math.sqrt(HEAD_DIM))
    k_all = l2n(conv_silu(1, k_ref, cwk_ref))
    v_all = conv_silu(2, v_ref, cwv_ref)

    neg_rate = -jnp.exp(jnp.full((1, GROUP), alog_ref[h], F32))
    dt_bias = jnp.full((1, GROUP), dtb_ref[h], F32)

    def groups(x):
        return [x[gi * GROUP:(gi + 1) * GROUP] for gi in range(ng)]

    a_rows = [a_ref[0, 0, pl.ds(s * ng + gi, 1), :] for gi in range(ng)]
    b_rows = [b_ref[0, 0, pl.ds(s * ng + gi, 1), :] for gi in range(ng)]
    outs, state = _gdn_groups(groups(q_all), groups(k_all), groups(v_all), a_rows, b_rows,
                              neg_rate, dt_bias, state_ref[...])
    state_ref[...] = state
    gain = gain_ref[...]
    for gi, o4 in enumerate(outs):
        lo = gi * GROUP
        gate = gate_ref[0, lo:lo + GROUP, :]
        o_ref[0, lo:lo + GROUP, :] = (_rms(o4, gain) * _silu(gate)).astype(o_ref.dtype)


def _gdn(proj_g, a_t, b_t, conv_w, a_log, dt_bias, gain, ts):
    b, s, _ = proj_g.shape
    nh = GDN_HEADS
    smem = pl.BlockSpec(memory_space=pltpu.SMEM)

    def col(group):
        return pl.BlockSpec((1, ts, HEAD_DIM), lambda bi, h, si: (bi, si, group * nh + h))

    def cw(group):
        return pl.BlockSpec((CONV_W, HEAD_DIM), lambda bi, h, si: (0, group * nh + h))

    ab = pl.BlockSpec((1, 1, s // GROUP, GROUP), lambda bi, h, si: (bi, h, 0, 0))
    return pl.pallas_call(
        functools.partial(_gdn_kernel, ts=ts),
        grid=(b, nh, s // ts),
        in_specs=[smem, smem, col(0), col(1), col(2), col(3), ab, ab,
                  cw(0), cw(1), cw(2),
                  pl.BlockSpec((1, HEAD_DIM), lambda bi, h, si: (0, 0))],
        out_specs=pl.BlockSpec((1, ts, HEAD_DIM), lambda bi, h, si: (bi, si, h)),
        out_shape=jax.ShapeDtypeStruct((b, s, D_GDN), BF16),
        scratch_shapes=[pltpu.VMEM((3, ts + 8, HEAD_DIM), F32),
                        pltpu.VMEM((HEAD_DIM, HEAD_DIM), F32)],
        compiler_params=_cparams(("parallel", "parallel", "arbitrary")),
        name="gdn",
    )(a_log, dt_bias, proj_g, proj_g, proj_g, proj_g, a_t, b_t,
      conv_w, conv_w, conv_w, gain)


def _pick(n, pref):
    t = min(pref, n)
    while n % t:
        t -= 128
    return t


def _pad_ff(w, axis, mult):
    pad = (-w.shape[axis]) % mult
    if pad == 0:
        return w
    widths = [(0, 0)] * w.ndim
    widths[axis] = (0, pad)
    return jnp.pad(w, widths)


def _ffn_weights(wg, wu, wd, tf):
    return (_pad_ff(wg.astype(BF16), 1, tf), _pad_ff(wu.astype(BF16), 1, tf),
            _pad_ff(wd.astype(BF16), 0, tf))


def kernel(x, ffn1_norm, ffn1_w_gate, ffn1_w_up, ffn1_w_down, mix_norm, w_in, sb_out_norm, conv_w, a_log, dt_bias, gdn_out_norm, w_out, ffn2_norm, ffn2_w_gate, ffn2_w_up, ffn2_w_down, final_norm):
    bsz, s_len, d = x.shape
    t = bsz * s_len
    assert ffn1_norm.shape[0] == 1, "single-layer block"
    l = 0
    tm = _pick(t, 512)
    tf = 512
    x2d = x.reshape(t, d)
    row = lambda v: v.reshape(1, -1)

    wg, wu, wd = _ffn_weights(ffn1_w_gate[l], ffn1_w_up[l], ffn1_w_down[l], tf)
    x1, hmix = _ffn(x2d, row(ffn1_norm[l]), wg, wu, wd, row(mix_norm[l]), "mix", tm, tf)

    w = w_in[l]
    o_qkv = 3 * D_SB
    o_a = o_qkv + 3 * D_GDN
    o_gate = o_a + 2 * GDN_HEADS
    w_sb = w[:, :o_qkv].astype(BF16)
    w_g = jnp.concatenate([w[:, o_qkv:o_a], w[:, o_gate:]], axis=1).astype(BF16)
    w_ab = jnp.pad(w[:, o_a:o_gate], ((0, 0), (0, HEAD_DIM - 2 * GDN_HEADS))).astype(BF16)
    tmm = _pick(t, 1024)
    proj_sb = _matmul(hmix, w_sb, BF16, tmm, 1024).reshape(bsz, s_len, 3 * D_SB)
    proj_g = _matmul(hmix, w_g, F32, tmm, 1024).reshape(bsz, s_len, 4 * D_GDN)
    proj_ab = _matmul(hmix, w_ab, F32, tmm, HEAD_DIM).reshape(bsz, s_len, HEAD_DIM)

    def per_head(cols):
        return cols.transpose(0, 2, 1).reshape(bsz, GDN_HEADS, s_len // GROUP, GROUP)

    a_t = per_head(proj_ab[..., :GDN_HEADS])
    b_t = per_head(proj_ab[..., GDN_HEADS:2 * GDN_HEADS])

    tq = _pick(s_len, 512)
    o_sb = _sb_attention(proj_sb, row(sb_out_norm[l]), tq, _pick(tq, 256))
    o_g = _gdn(proj_g, a_t, b_t, conv_w[l], a_log[l], dt_bias[l],
               row(gdn_out_norm[l]), _pick(s_len, 1024))

    x2 = _out_proj(o_sb.reshape(t, D_SB), o_g.reshape(t, D_GDN),
                   w_out[l].astype(BF16), x1, tmm, _pick(d, 1024))

    wg, wu, wd = _ffn_weights(ffn2_w_gate[l], ffn2_w_up[l], ffn2_w_down[l], tf)
    (out,) = _ffn(x2, row(ffn2_norm[l]), wg, wu, wd, row(final_norm), "final", tm, tf)
    return out.reshape(bsz, s_len, d)
```

```python
import functools
import math

import jax
import jax.numpy as jnp
from jax import lax
from jax.experimental import pallas as pl
from jax.experimental.pallas import tpu as pltpu

F32 = jnp.float32
BF16 = jnp.bfloat16
EPS = 1e-6

SB_HEADS = 8
GDN_HEADS = 8
HEAD_DIM = 128
CHUNK = 64
GROUP = 256
CONV_W = 4
D_SB = SB_HEADS * HEAD_DIM
D_GDN = GDN_HEADS * HEAD_DIM

VMEM_LIMIT_BYTES = 56 * 1024 * 1024
MASKED_LOG = -1e30


def _cparams(semantics):
    return pltpu.CompilerParams(dimension_semantics=semantics,
                                vmem_limit_bytes=VMEM_LIMIT_BYTES)


def _rms(x, gain):
    return x * lax.rsqrt(jnp.mean(x * x, axis=-1, keepdims=True) + EPS) * gain


def _sigmoid(x):
    return 1.0 / (1.0 + jnp.exp(-x))


def _silu(x):
    return x * _sigmoid(x)


def _dot(a, b):
    return jnp.dot(a, b, preferred_element_type=F32)


def _dot_nt(a, b):
    return lax.dot_general(a, b, (((1,), (1,)), ((), ())), preferred_element_type=F32)


def _split(x):
    hi = x.astype(BF16)
    lo = (x - hi.astype(F32)).astype(BF16)
    return hi, lo


def _ffn_kernel(x_ref, g_ref, wg_ref, wu_ref, wd_ref, wgt_ref, wut_ref, wdt_ref, g2_ref,
                *refs, n_main, post):
    if post == "mix":
        o_ref, h_ref, xn_ref, acc_ref = refs
    else:
        o_ref, xn_ref, acc_ref = refs
    f = pl.program_id(1)

    def accumulate(wg, wu, wd):
        xn = xn_ref[...]
        hid = (_silu(_dot(xn, wg)) * _dot(xn, wu)).astype(BF16)
        acc_ref[...] += _dot(hid, wd)

    @pl.when(f == 0)
    def _():
        xn_ref[...] = _rms(x_ref[...], g_ref[...]).astype(BF16)
        acc_ref[...] = jnp.zeros_like(acc_ref)

    @pl.when(f < n_main)
    def _():
        accumulate(wg_ref[...], wu_ref[...], wd_ref[...])

    @pl.when(f == n_main)
    def _():
        accumulate(wgt_ref[...], wut_ref[...], wdt_ref[...])
        y = x_ref[...] + 0.5 * acc_ref[...]
        if post == "mix":
            o_ref[...] = y
            h_ref[...] = _rms(y, g2_ref[...]).astype(BF16)
        else:
            o_ref[...] = _rms(y, g2_ref[...])


def _ffn(x, gain, wg, wu, wd, gain2, post, tm, tf):
    t, d = x.shape
    f_dim = wg.shape[1]
    n_main = (f_dim - 1) // tf
    cut = n_main * tf
    tail = f_dim - cut
    last = n_main - 1
    out_shape = [jax.ShapeDtypeStruct((t, d), F32)]
    out_specs = [pl.BlockSpec((tm, d), lambda i, f: (i, 0))]
    if post == "mix":
        out_shape.append(jax.ShapeDtypeStruct((t, d), BF16))
        out_specs.append(pl.BlockSpec((tm, d), lambda i, f: (i, 0)))
    res = pl.pallas_call(
        functools.partial(_ffn_kernel, n_main=n_main, post=post),
        grid=(t // tm, n_main + 1),
        in_specs=[
            pl.BlockSpec((tm, d), lambda i, f: (i, 0)),
            pl.BlockSpec((1, d), lambda i, f: (0, 0)),
            pl.BlockSpec((d, tf), lambda i, f: (0, jnp.minimum(f, last))),
            pl.BlockSpec((d, tf), lambda i, f: (0, jnp.minimum(f, last))),
            pl.BlockSpec((tf, d), lambda i, f: (jnp.minimum(f, last), 0)),
            pl.BlockSpec((d, tail), lambda i, f: (0, 0)),
            pl.BlockSpec((d, tail), lambda i, f: (0, 0)),
            pl.BlockSpec((tail, d), lambda i, f: (0, 0)),
            pl.BlockSpec((1, d), lambda i, f: (0, 0)),
        ],
        out_specs=out_specs,
        out_shape=out_shape,
        scratch_shapes=[pltpu.VMEM((tm, d), BF16), pltpu.VMEM((tm, d), F32)],
        compiler_params=_cparams(("parallel", "arbitrary")),
        name="ffn_" + post,
    )(x, gain, wg, wu, wd, wg[:, cut:], wu[:, cut:], wd[cut:], gain2)
    return res


def _mm_kernel(x_ref, w_ref, o_ref):
    o_ref[...] = _dot(x_ref[...], w_ref[...]).astype(o_ref.dtype)


def _matmul(x, w, out_dtype, tm, tn):
    m, k = x.shape
    n = w.shape[1]
    return pl.pallas_call(
        _mm_kernel,
        grid=(m // tm, n // tn),
        in_specs=[pl.BlockSpec((tm, k), lambda i, j: (i, 0)),
                  pl.BlockSpec((k, tn), lambda i, j: (0, j))],
        out_specs=pl.BlockSpec((tm, tn), lambda i, j: (i, j)),
        out_shape=jax.ShapeDtypeStruct((m, n), out_dtype),
        compiler_params=_cparams(("parallel", "parallel")),
        name="in_proj_n%d" % n,
    )(x, w)


def _out_proj_kernel(ya_ref, yb_ref, wa_ref, wb_ref, r_ref, o_ref):
    o_ref[...] = (r_ref[...] + _dot(ya_ref[...], wa_ref[...])
                  + _dot(yb_ref[...], wb_ref[...]))


def _out_proj(ya, yb, w, resid, tm, tn):
    m, ka = ya.shape
    kb = yb.shape[1]
    n = w.shape[1]
    assert ka == kb
    return pl.pallas_call(
        _out_proj_kernel,
        grid=(m // tm, n // tn),
        in_specs=[pl.BlockSpec((tm, ka), lambda i, j: (i, 0)),
                  pl.BlockSpec((tm, kb), lambda i, j: (i, 0)),
                  pl.BlockSpec((ka, tn), lambda i, j: (0, j)),
                  pl.BlockSpec((kb, tn), lambda i, j: (1, j)),
                  pl.BlockSpec((tm, tn), lambda i, j: (i, j))],
        out_specs=pl.BlockSpec((tm, tn), lambda i, j: (i, j)),
        out_shape=jax.ShapeDtypeStruct((m, n), F32),
        compiler_params=_cparams(("parallel", "parallel")),
        name="out_proj",
    )(ya, yb, w, w, resid)


def _sb_kernel(q_ref, k_ref, v_ref, gain_ref, o_ref,
               acc_ref, run_ref, lb_ref, hi_ref, lo_ref, rs_ref, *, tq, tk, scale):
    assert tq == 2 * tk
    i = pl.program_id(2)
    sub = HEAD_DIM
    nsub = tq // sub
    row = lax.broadcasted_iota(jnp.int32, (sub, tk), 0)
    col = lax.broadcasted_iota(jnp.int32, (sub, tk), 1)
    r2 = lax.broadcasted_iota(jnp.int32, (tk, tk), 0)
    c2 = lax.broadcasted_iota(jnp.int32, (tk, tk), 1)
    later = (r2 > c2).astype(BF16)
    acc_ref[...] = jnp.zeros_like(acc_ref)
    run_ref[...] = jnp.zeros_like(run_ref)

    def rows(r):
        return slice(r * sub, (r + 1) * sub)

    def mask_kind(diag_off, r):
        if diag_off is None or diag_off + tk - 1 < r * sub:
            return "all"
        if diag_off >= (r + 1) * sub - 1:
            return "none"
        return "part"

    def key_rows(j):
        return pl.ds(pl.multiple_of(j * tk, tk), tk)

    def matmuls(weigh_blk, score_blk, r):
        z = sfx = None
        if score_blk is not None and mask_kind(score_blk[2], r) != "none":
            z = _dot_nt(q_ref[0, rows(r), :], k_ref[0, key_rows(score_blk[0]), :])
        if weigh_blk is not None and mask_kind(weigh_blk[2], r) != "none":
            slot_w = weigh_blk[1]
            sfx = (_dot(hi_ref[slot_w, rows(r), :], later)
                   + _dot(lo_ref[slot_w, rows(r), :], later))
        return z, sfx

    def elementwise(weigh_blk, score_blk, r, z, sfx):
        if score_blk is not None and mask_kind(score_blk[2], r) != "none":
            _, slot_s, off_s = score_blk
            z = z * scale
            log_beta = jnp.minimum(z, 0.0) - jnp.log(1.0 + jnp.exp(-jnp.abs(z)))
            log_keep = log_beta - z
            if mask_kind(off_s, r) == "part":
                mask = (col + off_s) < (row + r * sub)
                log_keep = jnp.where(mask, log_keep, 0.0)
                log_beta = jnp.where(mask, log_beta, MASKED_LOG)
            hi, lo = _split(log_keep)
            lb_ref[slot_s, rows(r), :] = log_beta
            hi_ref[slot_s, rows(r), :] = hi
            lo_ref[slot_s, rows(r), :] = lo
            rs_ref[slot_s, rows(r), :] = jnp.broadcast_to(
                jnp.sum(log_keep, axis=-1, keepdims=True), (sub, HEAD_DIM))
        if weigh_blk is not None and mask_kind(weigh_blk[2], r) != "none":
            jw, slot_w, _ = weigh_blk
            run = jnp.concatenate([run_ref[rows(r), :]] * (tk // HEAD_DIM), axis=1)
            w = jnp.exp(lb_ref[slot_w, rows(r), :] + sfx + run)
            acc_ref[rows(r), :] += _dot(w.astype(BF16), v_ref[0, key_rows(jw), :])
            run_ref[rows(r), :] += rs_ref[slot_w, rows(r), :]

    def step(first, weigh_blk, score_blk, nxt):
        cur = matmuls(weigh_blk, score_blk, 0) if first is None else first
        for r in range(nsub):
            if r + 1 < nsub:
                ahead = matmuls(weigh_blk, score_blk, r + 1)
            else:
                ahead = matmuls(nxt[0], nxt[1], 0) if nxt is not None else None
            elementwise(weigh_blk, score_blk, r, *cur)
            cur = ahead
        return cur

    step(None, None, (2 * i + 1, 0, tk), None)
    first = step(None, (2 * i + 1, 0, tk), (2 * i, 1, 0),
                 ((2 * i, 1, None), (jnp.maximum(2 * i - 1, 0), 0, None)))

    def body(p, first):
        j = 2 * (i - p)
        first = step(first, (j, 1, None), (j - 1, 0, None),
                     ((j - 1, 0, None), (j - 2, 1, None)))
        return step(first, (j - 1, 0, None), (j - 2, 1, None),
                    ((j - 2, 1, None), (jnp.maximum(j - 3, 0), 0, None)))

    first = lax.fori_loop(0, i, body, first)
    step(first, (0, 1, None), None, None)
    o_ref[0] = _rms(acc_ref[...], gain_ref[...]).astype(o_ref.dtype)


def _sb_attention(proj_sb, gain, tq, tk):
    b, s, _ = proj_sb.shape
    nh = SB_HEADS
    return pl.pallas_call(
        functools.partial(_sb_kernel, tq=tq, tk=tk, scale=1.0 / math.sqrt(HEAD_DIM)),
        grid=(b, nh, s // tq),
        in_specs=[
            pl.BlockSpec((1, tq, HEAD_DIM), lambda bi, h, i: (bi, i, h)),
            pl.BlockSpec((1, s, HEAD_DIM), lambda bi, h, i: (bi, 0, nh + h)),
            pl.BlockSpec((1, s, HEAD_DIM), lambda bi, h, i: (bi, 0, 2 * nh + h)),
            pl.BlockSpec((1, HEAD_DIM), lambda bi, h, i: (0, 0)),
        ],
        out_specs=pl.BlockSpec((1, tq, HEAD_DIM), lambda bi, h, i: (bi, i, h)),
        out_shape=jax.ShapeDtypeStruct((b, s, D_SB), BF16),
        scratch_shapes=[pltpu.VMEM((tq, HEAD_DIM), F32), pltpu.VMEM((tq, HEAD_DIM), F32),
                        pltpu.VMEM((2, tq, tk), F32), pltpu.VMEM((2, tq, tk), BF16),
                        pltpu.VMEM((2, tq, tk), BF16), pltpu.VMEM((2, tq, HEAD_DIM), F32)],
        compiler_params=_cparams(("parallel", "parallel", "arbitrary")),
        name="sb_attn",
    )(proj_sb, proj_sb, proj_sb, gain)


def _dot3(a_hi, a_lo, b_hi, b_lo):
    return _dot(a_hi, b_hi) + (_dot(a_lo, b_hi) + _dot(a_hi, b_lo))


def _unit_lower_inverse(a_bds, same_chunk):
    n = GROUP // CHUNK
    r = lax.broadcasted_iota(jnp.int32, (CHUNK, GROUP), 0)
    cl = lax.broadcasted_iota(jnp.int32, (CHUNK, GROUP), 1) & (CHUNK - 1)
    same16 = (r >> 4) == (cl >> 4)
    same32 = (r >> 5) == (cl >> 5)
    eye_p = (r == cl).astype(F32)

    def packed(a_bd):
        a_p = a_bd[0:CHUNK]
        for c in range(1, n):
            a_p = a_p + a_bd[c * CHUNK:(c + 1) * CHUNK]
        return a_p

    def block_diag(x_p):
        return jnp.where(same_chunk, jnp.concatenate([x_p] * n, axis=0), jnp.zeros((), x_p.dtype))

    def mm(xs, ys):
        out = []
        for x_p, y_p in zip(xs, ys):
            xh, xl = _split(x_p)
            yh, yl = _split(y_p)
            out.append(_dot3(xh, xl, block_diag(yh), block_diag(yl)))
        return out

    def axpy(xs, ys, sign):
        return [x + y if sign > 0 else x - y for x, y in zip(xs, ys)]

    a_ps = [packed(a) for a in a_bds]
    a16 = [jnp.where(same16, a, 0.0) for a in a_ps]
    x = [eye_p - a for a in a16]
    p = mm(a16, a16)
    x = axpy(x, mm(x, p), 1.0)
    p = mm(p, p)
    x = axpy(x, mm(x, p), 1.0)
    p = mm(p, p)
    x = axpy(x, mm(x, p), 1.0)
    e32 = [jnp.where(same32 & ~same16, a, 0.0) for a in a_ps]
    x = axpy(x, mm(x, mm(e32, x)), -1.0)
    e64 = [jnp.where(same32, 0.0, a) for a in a_ps]
    x = axpy(x, mm(x, mm(e64, x)), -1.0)
    return [block_diag(xi) for xi in x]


def _gdn_groups(qs, ks, vs, a_rows, b_rows, neg_rates, dt_biases, states):
    n = GROUP // CHUNK
    ng = len(qs)
    steps = (ng // len(states)) * n
    shape = (GROUP, GROUP)
    r = lax.broadcasted_iota(jnp.int32, shape, 0)
    c = lax.broadcasted_iota(jnp.int32, shape, 1)
    same_chunk = (r >> 6) == (c >> 6)
    lower_incl = same_chunk & (c <= r)
    upper_incl = same_chunk & (r <= c)
    strict = same_chunk & (c < r)
    eye = r == c

    def rowsum(x):
        return jnp.sum(x, axis=1, keepdims=True)

    def colsum(x):
        return jnp.sum(x, axis=0, keepdims=True)

    decay, exp_gc, kend_scale, glast_rows, beta_cols = [], [], [], [], []
    for a_row, b_row, neg_rate, dt_bias in zip(a_rows, b_rows, neg_rates, dt_biases):
        sp_in = a_row + dt_bias
        g_row = neg_rate * (jnp.maximum(sp_in, 0.0) + jnp.log(1.0 + jnp.exp(-jnp.abs(sp_in))))
        g_b = jnp.broadcast_to(g_row, shape)
        gc_col = rowsum(jnp.where(lower_incl, g_b, 0.0))
        glast_col = rowsum(jnp.where(same_chunk, g_b, 0.0))
        g_col = rowsum(jnp.where(eye, g_b, 0.0))
        beta_cols.append(rowsum(jnp.where(eye, jnp.broadcast_to(_sigmoid(b_row), shape), 0.0)))
        gcol_b = jnp.broadcast_to(g_col, shape)
        gc_row = colsum(jnp.where(upper_incl, gcol_b, 0.0))
        glast_rows.append(colsum(jnp.where(same_chunk, gcol_b, 0.0)))
        decay.append(jnp.where(lower_incl,
                               jnp.exp(jnp.where(lower_incl, gc_col - gc_row, 0.0)), 0.0))
        exp_gc.append(jnp.exp(gc_col))
        kend_scale.append(jnp.exp(glast_col - gc_col))

    kbs = [k * b for k, b in zip(ks, beta_cols)]
    grams = [_dot_nt(jnp.concatenate([kb, q], axis=0).astype(BF16), k.astype(BF16))
             for kb, q, k in zip(kbs, qs, ks)]
    a_bds = [jnp.where(strict, g[:GROUP] * d, 0.0) for g, d in zip(grams, decay)]
    aqk_bds = [(g[GROUP:] * d).astype(BF16) for g, d in zip(grams, decay)]

    tinvs = _unit_lower_inverse(a_bds, same_chunk)
    wus = []
    for tinv, kb, v, eg, bc in zip(tinvs, kbs, vs, exp_gc, beta_cols):
        th, tl = _split(tinv)
        rh, rl = _split(jnp.concatenate([kb * eg, v * bc], axis=1))
        wus.append(_dot3(th, tl, rh, rl))

    lane_chunk = lax.broadcasted_iota(jnp.int32, (HEAD_DIM, GROUP), 1) >> 6
    lane = lax.broadcasted_iota(jnp.int32, (1, GROUP), 1)
    pns, decs = [], []
    for k, scale, wu, gl in zip(ks, kend_scale, wus, glast_rows):
        kend_t = (k * scale).T
        wu_b = wu.astype(BF16)
        for ci in range(n):
            pns.append(_dot(jnp.where(lane_chunk == ci, kend_t, 0.0).astype(BF16), wu_b))
            decs.append(jnp.exp(jnp.sum(jnp.where(lane == ci * CHUNK, gl, 0.0),
                                        axis=1, keepdims=True)))

    states = list(states)
    chunk_states = [None] * len(pns)
    for t in range(steps):
        for hh in range(len(states)):
            idx = hh * steps + t
            pn = pns[idx]
            chunk_states[idx] = states[hh].astype(BF16)
            states[hh] = ((decs[idx] * states[hh] + pn[:, HEAD_DIM:])
                          - _dot(pn[:, :HEAD_DIM].astype(BF16), chunk_states[idx]))

    outs = []
    for gi in range(ng):
        w4 = wus[gi][:, :HEAD_DIM]
        u4 = wus[gi][:, HEAD_DIM:]
        qdec4 = qs[gi] * exp_gc[gi]
        v_new, o_q = [], []
        for ci in range(n):
            lo = ci * CHUNK
            lhs = jnp.concatenate([w4[lo:lo + CHUNK], qdec4[lo:lo + CHUNK]], axis=0).astype(BF16)
            res = _dot(lhs, chunk_states[gi * n + ci])
            v_new.append(u4[lo:lo + CHUNK] - res[:CHUNK])
            o_q.append(res[CHUNK:])
        v_new = jnp.concatenate(v_new, axis=0).astype(BF16)
        outs.append(jnp.concatenate(o_q, axis=0) + _dot(aqk_bds[gi], v_new))
    return outs, states


def _gdn_kernel(alog_ref, dtb_ref, q_ref, k_ref, v_ref, gate_ref, a_ref, b_ref,
                cwq_ref, cwk_ref, cwv_ref, gain_ref, o_ref,
                ext_ref, state_ref, *, ts, hps):
    hp = pl.program_id(1)
    s = pl.program_id(2)
    ng = ts // GROUP
    pad = 8

    @pl.when(s == 0)
    def _():
        ext_ref[:, 0:pad, :] = jnp.zeros((3, pad, hps * HEAD_DIM), F32)
        state_ref[...] = jnp.zeros_like(state_ref)

    def conv_silu(idx, x_ref, cw_ref):
        ext_ref[idx, pad:pad + ts, :] = x_ref[0]
        acc = None
        for tap in range(CONV_W):
            off = pad - (CONV_W - 1) + tap
            term = cw_ref[tap:tap + 1, :] * ext_ref[idx, off:off + ts, :]
            acc = term if acc is None else acc + term
        ext_ref[idx, 0:pad, :] = ext_ref[idx, ts:ts + pad, :]
        return _silu(acc)

    def l2n(x):
        return x * lax.rsqrt(jnp.sum(x * x, axis=-1, keepdims=True) + EPS)

    def groups(x, hh, fn=None):
        out = []
        for gi in range(ng):
            blk = x[gi * GROUP:(gi + 1) * GROUP, hh * HEAD_DIM:(hh + 1) * HEAD_DIM]
            out.append(blk if fn is None else fn(blk))
        return out

    q_all = conv_silu(0, q_ref, cwq_ref)
    k_all = conv_silu(1, k_ref, cwk_ref)
    v_all = conv_silu(2, v_ref, cwv_ref)
    qs, ks, vs, a_rows, b_rows, neg_rates, dt_biases = [], [], [], [], [], [], []
    for hh in range(hps):
        qs += groups(q_all, hh, lambda x: l2n(x) * (1.0 / math.sqrt(HEAD_DIM)))
        ks += groups(k_all, hh, l2n)
        vs += groups(v_all, hh)
        a_rows += [a_ref[0, hh, pl.ds(s * ng + gi, 1), :] for gi in range(ng)]
        b_rows += [b_ref[0, hh, pl.ds(s * ng + gi, 1), :] for gi in range(ng)]
        neg_rates += [-jnp.exp(jnp.full((1, GROUP), alog_ref[hp * hps + hh], F32))] * ng
        dt_biases += [jnp.full((1, GROUP), dtb_ref[hp * hps + hh], F32)] * ng

    outs, states = _gdn_groups(qs, ks, vs, a_rows, b_rows, neg_rates, dt_biases,
                               [state_ref[hh] for hh in range(hps)])
    gain = gain_ref[...]
    for hh in range(hps):
        state_ref[hh] = states[hh]
        lanes = slice(hh * HEAD_DIM, (hh + 1) * HEAD_DIM)
        for gi in range(ng):
            lo = gi * GROUP
            gate = gate_ref[0, lo:lo + GROUP, lanes]
            o_ref[0, lo:lo + GROUP, lanes] = (_rms(outs[hh * ng + gi], gain)
                                              * _silu(gate)).astype(o_ref.dtype)


def _gdn(proj_g, a_t, b_t, conv_w, a_log, dt_bias, gain, ts, hps):
    b, s, _ = proj_g.shape
    nhp = GDN_HEADS // hps
    width = hps * HEAD_DIM
    smem = pl.BlockSpec(memory_space=pltpu.SMEM)

    def col(group):
        return pl.BlockSpec((1, ts, width), lambda bi, h, si: (bi, si, group * nhp + h))

    def cw(group):
        return pl.BlockSpec((CONV_W, width), lambda bi, h, si: (0, group * nhp + h))

    ab = pl.BlockSpec((1, hps, s // GROUP, GROUP), lambda bi, h, si: (bi, h, 0, 0))
    return pl.pallas_call(
        functools.partial(_gdn_kernel, ts=ts, hps=hps),
        grid=(b, nhp, s // ts),
        in_specs=[smem, smem, col(0), col(1), col(2), col(3), ab, ab,
                  cw(0), cw(1), cw(2),
                  pl.BlockSpec((1, HEAD_DIM), lambda bi, h, si: (0, 0))],
        out_specs=pl.BlockSpec((1, ts, width), lambda bi, h, si: (bi, si, h)),
        out_shape=jax.ShapeDtypeStruct((b, s, D_GDN), BF16),
        scratch_shapes=[pltpu.VMEM((3, ts + 8, width), F32),
                        pltpu.VMEM((hps, HEAD_DIM, HEAD_DIM), F32)],
        compiler_params=_cparams(("parallel", "parallel", "arbitrary")),
        name="gdn",
    )(a_log, dt_bias, proj_g, proj_g, proj_g, proj_g, a_t, b_t,
      conv_w, conv_w, conv_w, gain)


def _pick(n, pref):
    t = min(pref, n)
    while n % t:
        t -= 128
    return t


def _ffn_weights(wg, wu, wd):
    return wg.astype(BF16), wu.astype(BF16), wd.astype(BF16)


def kernel(x, ffn1_norm, ffn1_w_gate, ffn1_w_up, ffn1_w_down, mix_norm, w_in, sb_out_norm, conv_w, a_log, dt_bias, gdn_out_norm, w_out, ffn2_norm, ffn2_w_gate, ffn2_w_up, ffn2_w_down, final_norm):
    bsz, s_len, d = x.shape
    t = bsz * s_len
    assert ffn1_norm.shape[0] == 1, "single-layer block"
    l = 0
    tm = _pick(t, 512)
    tf = 512
    x2d = x.reshape(t, d)
    row = lambda v: v.reshape(1, -1)

    wg, wu, wd = _ffn_weights(ffn1_w_gate[l], ffn1_w_up[l], ffn1_w_down[l])
    x1, hmix = _ffn(x2d, row(ffn1_norm[l]), wg, wu, wd, row(mix_norm[l]), "mix", tm, tf)

    w = w_in[l]
    o_qkv = 3 * D_SB
    o_a = o_qkv + 3 * D_GDN
    o_gate = o_a + 2 * GDN_HEADS
    w_sb = w[:, :o_qkv].astype(BF16)
    w_g = jnp.concatenate([w[:, o_qkv:o_a], w[:, o_gate:]], axis=1).astype(BF16)
    w_ab = jnp.pad(w[:, o_a:o_gate], ((0, 0), (0, HEAD_DIM - 2 * GDN_HEADS))).astype(BF16)
    tmm = _pick(t, 1024)
    proj_sb = _matmul(hmix, w_sb, BF16, tmm, 1024).reshape(bsz, s_len, 3 * D_SB)
    proj_g = _matmul(hmix, w_g, F32, tmm, 1024).reshape(bsz, s_len, 4 * D_GDN)
    proj_ab = _matmul(hmix, w_ab, F32, tmm, HEAD_DIM).reshape(bsz, s_len, HEAD_DIM)

    def per_head(cols):
        return cols.transpose(0, 2, 1).reshape(bsz, GDN_HEADS, s_len // GROUP, GROUP)

    a_t = per_head(proj_ab[..., :GDN_HEADS])
    b_t = per_head(proj_ab[..., GDN_HEADS:2 * GDN_HEADS])

    tq = _pick(s_len, 512)
    o_sb = _sb_attention(proj_sb, row(sb_out_norm[l]), tq, _pick(tq, 256))
    o_g = _gdn(proj_g, a_t, b_t, conv_w[l], a_log[l], dt_bias[l],
               row(gdn_out_norm[l]), _pick(s_len, 1024), 2)

    x2 = _out_proj(o_sb.reshape(t, D_SB), o_g.reshape(t, D_GDN),
                   w_out[l].astype(BF16), x1, tmm, _pick(d, 1024))

    wg, wu, wd = _ffn_weights(ffn2_w_gate[l], ffn2_w_up[l], ffn2_w_down[l])
    (out,) = _ffn(x2, row(ffn2_norm[l]), wg, wu, wd, row(final_norm), "final", tm, tf)
    return out.reshape(bsz, s_len, d)
```

```python
import functools
import math

import jax
import jax.numpy as jnp
from jax import lax
from jax.experimental import pallas as pl
from jax.experimental.pallas import tpu as pltpu

F32 = jnp.float32
BF16 = jnp.bfloat16
EPS = 1e-6

SB_HEADS = 8
GDN_HEADS = 8
HEAD_DIM = 128
CHUNK = 64
GROUP = 256
CONV_W = 4
D_SB = SB_HEADS * HEAD_DIM
D_GDN = GDN_HEADS * HEAD_DIM

VMEM_LIMIT_BYTES = 56 * 1024 * 1024
MASKED_LOG = -1e30


def _cparams(semantics):
    return pltpu.CompilerParams(dimension_semantics=semantics,
                                vmem_limit_bytes=VMEM_LIMIT_BYTES)


def _rms(x, gain):
    return x * lax.rsqrt(jnp.mean(x * x, axis=-1, keepdims=True) + EPS) * gain


def _sigmoid(x):
    return 1.0 / (1.0 + jnp.exp(-x))


def _silu(x):
    return x * _sigmoid(x)


def _dot(a, b):
    return jnp.dot(a, b, preferred_element_type=F32)


def _dot_nt(a, b):
    return lax.dot_general(a, b, (((1,), (1,)), ((), ())), preferred_element_type=F32)


def _split(x):
    hi = x.astype(BF16)
    lo = (x - hi.astype(F32)).astype(BF16)
    return hi, lo


def _ffn_kernel(x_ref, g_ref, wg_ref, wu_ref, wd_ref, wgt_ref, wut_ref, wdt_ref, g2_ref,
                *refs, n_main, post):
    if post == "mix":
        o_ref, h_ref, xn_ref, acc_ref = refs
    else:
        o_ref, xn_ref, acc_ref = refs
    f = pl.program_id(1)

    def accumulate(wg, wu, wd):
        xn = xn_ref[...]
        hid = (_silu(_dot(xn, wg)) * _dot(xn, wu)).astype(BF16)
        acc_ref[...] += _dot(hid, wd)

    @pl.when(f == 0)
    def _():
        xn_ref[...] = _rms(x_ref[...], g_ref[...]).astype(BF16)
        acc_ref[...] = jnp.zeros_like(acc_ref)

    @pl.when(f < n_main)
    def _():
        accumulate(wg_ref[...], wu_ref[...], wd_ref[...])

    @pl.when(f == n_main)
    def _():
        accumulate(wgt_ref[...], wut_ref[...], wdt_ref[...])
        y = x_ref[...] + 0.5 * acc_ref[...]
        if post == "mix":
            o_ref[...] = y
            h_ref[...] = _rms(y, g2_ref[...]).astype(BF16)
        else:
            o_ref[...] = _rms(y, g2_ref[...])


def _ffn(x, gain, wg, wu, wd, gain2, post, tm, tf):
    t, d = x.shape
    f_dim = wg.shape[1]
    n_main = (f_dim - 1) // tf
    cut = n_main * tf
    tail = f_dim - cut
    last = n_main - 1
    out_shape = [jax.ShapeDtypeStruct((t, d), F32)]
    out_specs = [pl.BlockSpec((tm, d), lambda i, f: (i, 0))]
    if post == "mix":
        out_shape.append(jax.ShapeDtypeStruct((t, d), BF16))
        out_specs.append(pl.BlockSpec((tm, d), lambda i, f: (i, 0)))
    res = pl.pallas_call(
        functools.partial(_ffn_kernel, n_main=n_main, post=post),
        grid=(t // tm, n_main + 1),
        in_specs=[
            pl.BlockSpec((tm, d), lambda i, f: (i, 0)),
            pl.BlockSpec((1, d), lambda i, f: (0, 0)),
            pl.BlockSpec((d, tf), lambda i, f: (0, jnp.minimum(f, last))),
            pl.BlockSpec((d, tf), lambda i, f: (0, jnp.minimum(f, last))),
            pl.BlockSpec((tf, d), lambda i, f: (jnp.minimum(f, last), 0)),
            pl.BlockSpec((d, tail), lambda i, f: (0, 0)),
            pl.BlockSpec((d, tail), lambda i, f: (0, 0)),
            pl.BlockSpec((tail, d), lambda i, f: (0, 0)),
            pl.BlockSpec((1, d), lambda i, f: (0, 0)),
        ],
        out_specs=out_specs,
        out_shape=out_shape,
        scratch_shapes=[pltpu.VMEM((tm, d), BF16), pltpu.VMEM((tm, d), F32)],
        compiler_params=_cparams(("parallel", "arbitrary")),
        name="ffn_" + post,
    )(x, gain, wg, wu, wd, wg[:, cut:], wu[:, cut:], wd[cut:], gain2)
    return res


def _mm_kernel(x_ref, w_ref, o_ref):
    o_ref[...] = _dot(x_ref[...], w_ref[...]).astype(o_ref.dtype)


def _matmul(x, w, out_dtype, tm, tn):
    m, k = x.shape
    n = w.shape[1]
    return pl.pallas_call(
        _mm_kernel,
        grid=(m // tm, n // tn),
        in_specs=[pl.BlockSpec((tm, k), lambda i, j: (i, 0)),
                  pl.BlockSpec((k, tn), lambda i, j: (0, j))],
        out_specs=pl.BlockSpec((tm, tn), lambda i, j: (i, j)),
        out_shape=jax.ShapeDtypeStruct((m, n), out_dtype),
        compiler_params=_cparams(("parallel", "parallel")),
        name="in_proj_n%d" % n,
    )(x, w)


def _out_proj_kernel(ya_ref, yb_ref, wa_ref, wb_ref, r_ref, o_ref):
    o_ref[...] = (r_ref[...] + _dot(ya_ref[...], wa_ref[...])
                  + _dot(yb_ref[...], wb_ref[...]))


def _out_proj(ya, yb, w, resid, tm, tn):
    m, ka = ya.shape
    kb = yb.shape[1]
    n = w.shape[1]
    assert ka == kb
    return pl.pallas_call(
        _out_proj_kernel,
        grid=(m // tm, n // tn),
        in_specs=[pl.BlockSpec((tm, ka), lambda i, j: (i, 0)),
                  pl.BlockSpec((tm, kb), lambda i, j: (i, 0)),
                  pl.BlockSpec((ka, tn), lambda i, j: (0, j)),
                  pl.BlockSpec((kb, tn), lambda i, j: (1, j)),
                  pl.BlockSpec((tm, tn), lambda i, j: (i, j))],
        out_specs=pl.BlockSpec((tm, tn), lambda i, j: (i, j)),
        out_shape=jax.ShapeDtypeStruct((m, n), F32),
        compiler_params=_cparams(("parallel", "parallel")),
        name="out_proj",
    )(ya, yb, w, w, resid)


def _sb_kernel(q_ref, k_ref, v_ref, gain_ref, o_ref,
               acc_ref, run_ref, lb_ref, hi_ref, lo_ref, rs_ref, *, tq, tk, scale):
    assert tq == 2 * tk
    i = pl.program_id(2)
    sub = HEAD_DIM
    nsub = tq // sub
    row = lax.broadcasted_iota(jnp.int32, (sub, tk), 0)
    col = lax.broadcasted_iota(jnp.int32, (sub, tk), 1)
    r2 = lax.broadcasted_iota(jnp.int32, (tk, tk), 0)
    c2 = lax.broadcasted_iota(jnp.int32, (tk, tk), 1)
    later = (r2 > c2).astype(BF16)
    acc_ref[...] = jnp.zeros_like(acc_ref)
    run_ref[...] = jnp.zeros_like(run_ref)

    def rows(r):
        return slice(r * sub, (r + 1) * sub)

    def mask_kind(diag_off, r):
        if diag_off is None or diag_off + tk - 1 < r * sub:
            return "all"
        if diag_off >= (r + 1) * sub - 1:
            return "none"
        return "part"

    def key_rows(j):
        return pl.ds(pl.multiple_of(j * tk, tk), tk)

    def matmuls(weigh_blk, score_blk, r):
        z = sfx = None
        if score_blk is not None and mask_kind(score_blk[2], r) != "none":
            z = _dot_nt(q_ref[0, rows(r), :], k_ref[0, key_rows(score_blk[0]), :])
        if weigh_blk is not None and mask_kind(weigh_blk[2], r) != "none":
            slot_w = weigh_blk[1]
            sfx = (_dot(hi_ref[slot_w, rows(r), :], later)
                   + _dot(lo_ref[slot_w, rows(r), :], later))
        return z, sfx

    def elementwise(weigh_blk, score_blk, r, z, sfx):
        if score_blk is not None and mask_kind(score_blk[2], r) != "none":
            _, slot_s, off_s = score_blk
            z = z * scale
            log_beta = jnp.minimum(z, 0.0) - jnp.log(1.0 + jnp.exp(-jnp.abs(z)))
            log_keep = log_beta - z
            if mask_kind(off_s, r) == "part":
                mask = (col + off_s) < (row + r * sub)
                log_keep = jnp.where(mask, log_keep, 0.0)
                log_beta = jnp.where(mask, log_beta, MASKED_LOG)
            hi, lo = _split(log_keep)
            lb_ref[slot_s, rows(r), :] = log_beta
            hi_ref[slot_s, rows(r), :] = hi
            lo_ref[slot_s, rows(r), :] = lo
            rs_ref[slot_s, rows(r), :] = jnp.broadcast_to(
                jnp.sum(log_keep, axis=-1, keepdims=True), (sub, HEAD_DIM))
        if weigh_blk is not None and mask_kind(weigh_blk[2], r) != "none":
            jw, slot_w, _ = weigh_blk
            run = jnp.concatenate([run_ref[rows(r), :]] * (tk // HEAD_DIM), axis=1)
            w = jnp.exp(lb_ref[slot_w, rows(r), :] + sfx + run)
            acc_ref[rows(r), :] += _dot(w.astype(BF16), v_ref[0, key_rows(jw), :])
            run_ref[rows(r), :] += rs_ref[slot_w, rows(r), :]

    def step(first, weigh_blk, score_blk, nxt):
        cur = matmuls(weigh_blk, score_blk, 0) if first is None else first
        for r in range(nsub):
            if r + 1 < nsub:
                ahead = matmuls(weigh_blk, score_blk, r + 1)
            else:
                ahead = matmuls(nxt[0], nxt[1], 0) if nxt is not None else None
            elementwise(weigh_blk, score_blk, r, *cur)
            cur = ahead
        return cur

    step(None, None, (2 * i + 1, 0, tk), None)
    first = step(None, (2 * i + 1, 0, tk), (2 * i, 1, 0),
                 ((2 * i, 1, None), (jnp.maximum(2 * i - 1, 0), 0, None)))

    def body(p, first):
        j = 2 * (i - p)
        first = step(first, (j, 1, None), (j - 1, 0, None),
                     ((j - 1, 0, None), (j - 2, 1, None)))
        return step(first, (j - 1, 0, None), (j - 2, 1, None),
                    ((j - 2, 1, None), (jnp.maximum(j - 3, 0), 0, None)))

    first = lax.fori_loop(0, i, body, first)
    step(first, (0, 1, None), None, None)
    o_ref[0] = _rms(acc_ref[...], gain_ref[...]).astype(o_ref.dtype)


def _sb_attention(proj_sb, gain, tq, tk):
    b, s, _ = proj_sb.shape
    nh = SB_HEADS
    return pl.pallas_call(
        functools.partial(_sb_kernel, tq=tq, tk=tk, scale=1.0 / math.sqrt(HEAD_DIM)),
        grid=(b, nh, s // tq),
        in_specs=[
            pl.BlockSpec((1, tq, HEAD_DIM), lambda bi, h, i: (bi, i, h)),
            pl.BlockSpec((1, s, HEAD_DIM), lambda bi, h, i: (bi, 0, nh + h)),
            pl.BlockSpec((1, s, HEAD_DIM), lambda bi, h, i: (bi, 0, 2 * nh + h)),
            pl.BlockSpec((1, HEAD_DIM), lambda bi, h, i: (0, 0)),
        ],
        out_specs=pl.BlockSpec((1, tq, HEAD_DIM), lambda bi, h, i: (bi, i, h)),
        out_shape=jax.ShapeDtypeStruct((b, s, D_SB), BF16),
        scratch_shapes=[pltpu.VMEM((tq, HEAD_DIM), F32), pltpu.VMEM((tq, HEAD_DIM), F32),
                        pltpu.VMEM((2, tq, tk), F32), pltpu.VMEM((2, tq, tk), BF16),
                        pltpu.VMEM((2, tq, tk), BF16), pltpu.VMEM((2, tq, HEAD_DIM), F32)],
        compiler_params=_cparams(("parallel", "parallel", "arbitrary")),
        name="sb_attn",
    )(proj_sb, proj_sb, proj_sb, gain)


def _dot3(a_hi, a_lo, b_hi, b_lo):
    return _dot(a_hi, b_hi) + (_dot(a_lo, b_hi) + _dot(a_hi, b_lo))


def _unit_lower_inverse(a_bds, same_chunk):
    n = GROUP // CHUNK
    r = lax.broadcasted_iota(jnp.int32, (CHUNK, GROUP), 0)
    cl = lax.broadcasted_iota(jnp.int32, (CHUNK, GROUP), 1) & (CHUNK - 1)
    same16 = (r >> 4) == (cl >> 4)
    same32 = (r >> 5) == (cl >> 5)
    eye_p = (r == cl).astype(F32)

    def packed(a_bd):
        a_p = a_bd[0:CHUNK]
        for c in range(1, n):
            a_p = a_p + a_bd[c * CHUNK:(c + 1) * CHUNK]
        return a_p

    def block_diag(x_p):
        return jnp.concatenate([x_p] * n, axis=0) * same_chunk().astype(x_p.dtype)

    def mm(xs, ys):
        out = []
        for x_p, y_p in zip(xs, ys):
            xh, xl = _split(x_p)
            yh, yl = _split(y_p)
            out.append(_dot3(xh, xl, block_diag(yh), block_diag(yl)))
        return out

    def mm1(xs, ys):
        return [_dot(x_p.astype(BF16), block_diag(y_p.astype(BF16))) for x_p, y_p in zip(xs, ys)]

    def axpy(xs, ys, sign):
        return [x + y if sign > 0 else x - y for x, y in zip(xs, ys)]

    a_ps = [packed(a) for a in a_bds]
    a16 = [jnp.where(same16, a, 0.0) for a in a_ps]
    x = [eye_p - a for a in a16]
    p = mm1(a16, a16)
    x = axpy(x, mm1(x, p), 1.0)
    p = mm1(p, p)
    x = axpy(x, mm1(x, p), 1.0)
    p = mm1(p, p)
    x = axpy(x, mm1(x, p), 1.0)
    e32 = [jnp.where(same32 & ~same16, a, 0.0) for a in a_ps]
    x = axpy(x, mm(x, mm(e32, x)), -1.0)
    e64 = [jnp.where(same32, 0.0, a) for a in a_ps]
    x = axpy(x, mm(x, mm(e64, x)), -1.0)
    return [block_diag(xi) for xi in x]


M_SAME, M_LOWER, M_UPPER, M_EYE, M_STRICT = range(5)


def _gdn_masks():
    shape = (GROUP, GROUP)
    r = lax.broadcasted_iota(jnp.int32, shape, 0)
    c = lax.broadcasted_iota(jnp.int32, shape, 1)
    same = (r >> 6) == (c >> 6)
    return [m.astype(F32) for m in
            (same, same & (c <= r), same & (r <= c), r == c, same & (c < r))]


def _gdn_groups(qs, ks, vs, a_rows, b_rows, neg_rates, dt_biases, states, mask_ref):
    n = GROUP // CHUNK
    ng = len(qs)
    steps = (ng // len(states)) * n
    shape = (GROUP, GROUP)

    def mask(m):
        return mask_ref[m]

    def rowsum(x):
        return jnp.sum(x, axis=1, keepdims=True)

    def colsum(x):
        return jnp.sum(x, axis=0, keepdims=True)

    decay, exp_gc, kend_scale, glast_rows, beta_cols = [], [], [], [], []
    for a_row, b_row, neg_rate, dt_bias in zip(a_rows, b_rows, neg_rates, dt_biases):
        sp_in = a_row + dt_bias
        g_row = neg_rate * (jnp.maximum(sp_in, 0.0) + jnp.log(1.0 + jnp.exp(-jnp.abs(sp_in))))
        g_b = jnp.broadcast_to(g_row, shape)
        gc_col = rowsum(g_b * mask(M_LOWER))
        glast_col = rowsum(g_b * mask(M_SAME))
        g_col = rowsum(g_b * mask(M_EYE))
        beta_cols.append(rowsum(jnp.broadcast_to(_sigmoid(b_row), shape) * mask(M_EYE)))
        gcol_b = jnp.broadcast_to(g_col, shape)
        gc_row = colsum(gcol_b * mask(M_UPPER))
        glast_rows.append(colsum(gcol_b * mask(M_SAME)))
        decay.append(jnp.exp(jnp.where(mask(M_LOWER) > 0.5, gc_col - gc_row, MASKED_LOG)))
        exp_gc.append(jnp.exp(gc_col))
        kend_scale.append(jnp.exp(glast_col - gc_col))

    kbs = [k * b for k, b in zip(ks, beta_cols)]
    grams = [_dot_nt(jnp.concatenate([kb, q], axis=0).astype(BF16), k.astype(BF16))
             for kb, q, k in zip(kbs, qs, ks)]
    a_bds = [g[:GROUP] * d * mask(M_STRICT) for g, d in zip(grams, decay)]
    aqk_bds = [(g[GROUP:] * d).astype(BF16) for g, d in zip(grams, decay)]

    tinvs = _unit_lower_inverse(a_bds, functools.partial(mask, M_SAME))
    wus = []
    for tinv, kb, v, eg, bc in zip(tinvs, kbs, vs, exp_gc, beta_cols):
        th, tl = _split(tinv)
        rh, rl = _split(jnp.concatenate([kb * eg, v * bc], axis=1))
        wus.append(_dot3(th, tl, rh, rl))

    lane_chunk = lax.broadcasted_iota(jnp.int32, (HEAD_DIM, GROUP), 1) >> 6
    lane = lax.broadcasted_iota(jnp.int32, (1, GROUP), 1)
    pns, decs = [], []
    for k, scale, wu, gl in zip(ks, kend_scale, wus, glast_rows):
        kend_t = (k * scale).T
        wu_b = wu.astype(BF16)
        for ci in range(n):
            pns.append(_dot(jnp.where(lane_chunk == ci, kend_t, 0.0).astype(BF16), wu_b))
            decs.append(jnp.exp(jnp.sum(jnp.where(lane == ci * CHUNK, gl, 0.0),
                                        axis=1, keepdims=True)))

    states = list(states)
    chunk_states = [None] * len(pns)
    for t in range(steps):
        for hh in range(len(states)):
            idx = hh * steps + t
            pn = pns[idx]
            chunk_states[idx] = states[hh].astype(BF16)
            states[hh] = ((decs[idx] * states[hh] + pn[:, HEAD_DIM:])
                          - _dot(pn[:, :HEAD_DIM].astype(BF16), chunk_states[idx]))

    outs = []
    for gi in range(ng):
        w4 = wus[gi][:, :HEAD_DIM]
        u4 = wus[gi][:, HEAD_DIM:]
        qdec4 = qs[gi] * exp_gc[gi]
        v_new, o_q = [], []
        for ci in range(n):
            lo = ci * CHUNK
            lhs = jnp.concatenate([w4[lo:lo + CHUNK], qdec4[lo:lo + CHUNK]], axis=0).astype(BF16)
            res = _dot(lhs, chunk_states[gi * n + ci])
            v_new.append(u4[lo:lo + CHUNK] - res[:CHUNK])
            o_q.append(res[CHUNK:])
        v_new = jnp.concatenate(v_new, axis=0).astype(BF16)
        outs.append(jnp.concatenate(o_q, axis=0) + _dot(aqk_bds[gi], v_new))
    return outs, states


def _gdn_kernel(alog_ref, dtb_ref, q_ref, k_ref, v_ref, gate_ref, a_ref, b_ref,
                cwq_ref, cwk_ref, cwv_ref, gain_ref, o_ref,
                ext_ref, state_ref, mask_ref, *, ts, hps):
    hp = pl.program_id(1)
    s = pl.program_id(2)
    ng = ts // GROUP
    pad = 8

    @pl.when(s == 0)
    def _():
        ext_ref[:, 0:pad, :] = jnp.zeros((3, pad, hps * HEAD_DIM), F32)
        state_ref[...] = jnp.zeros_like(state_ref)
        for m, mask in enumerate(_gdn_masks()):
            mask_ref[m] = mask

    def conv_silu(idx, x_ref, cw_ref):
        ext_ref[idx, pad:pad + ts, :] = x_ref[0]
        acc = None
        for tap in range(CONV_W):
            off = pad - (CONV_W - 1) + tap
            term = cw_ref[tap:tap + 1, :] * ext_ref[idx, off:off + ts, :]
            acc = term if acc is None else acc + term
        ext_ref[idx, 0:pad, :] = ext_ref[idx, ts:ts + pad, :]
        return _silu(acc)

    ones = jnp.ones((HEAD_DIM, HEAD_DIM), BF16)

    def l2n(x):
        hi, lo = _split(x * x)
        sumsq = _dot(hi, ones) + _dot(lo, ones)
        return x * lax.rsqrt(sumsq + EPS)

    def groups(x, hh, fn=None):
        out = []
        for gi in range(ng):
            blk = x[gi * GROUP:(gi + 1) * GROUP, hh * HEAD_DIM:(hh + 1) * HEAD_DIM]
            out.append(blk if fn is None else fn(blk))
        return out

    q_all = conv_silu(0, q_ref, cwq_ref)
    k_all = conv_silu(1, k_ref, cwk_ref)
    v_all = conv_silu(2, v_ref, cwv_ref)
    qs, ks, vs, a_rows, b_rows, neg_rates, dt_biases = [], [], [], [], [], [], []
    for hh in range(hps):
        qs += groups(q_all, hh, lambda x: l2n(x) * (1.0 / math.sqrt(HEAD_DIM)))
        ks += groups(k_all, hh, l2n)
        vs += groups(v_all, hh)
        a_rows += [a_ref[0, hh, pl.ds(s * ng + gi, 1), :] for gi in range(ng)]
        b_rows += [b_ref[0, hh, pl.ds(s * ng + gi, 1), :] for gi in range(ng)]
        neg_rates += [-jnp.exp(jnp.full((1, GROUP), alog_ref[hp * hps + hh], F32))] * ng
        dt_biases += [jnp.full((1, GROUP), dtb_ref[hp * hps + hh], F32)] * ng

    outs, states = _gdn_groups(qs, ks, vs, a_rows, b_rows, neg_rates, dt_biases,
                               [state_ref[hh] for hh in range(hps)], mask_ref)
    gain = gain_ref[...]
    for hh in range(hps):
        state_ref[hh] = states[hh]
        lanes = slice(hh * HEAD_DIM, (hh + 1) * HEAD_DIM)
        for gi in range(ng):
            lo = gi * GROUP
            gate = gate_ref[0, lo:lo + GROUP, lanes]
            o_ref[0, lo:lo + GROUP, lanes] = (_rms(outs[hh * ng + gi], gain)
                                              * _silu(gate)).astype(o_ref.dtype)


def _gdn(proj_g, a_t, b_t, conv_w, a_log, dt_bias, gain, ts, hps):
    b, s, _ = proj_g.shape
    nhp = GDN_HEADS // hps
    width = hps * HEAD_DIM
    smem = pl.BlockSpec(memory_space=pltpu.SMEM)

    def col(group):
        return pl.BlockSpec((1, ts, width), lambda bi, h, si: (bi, si, group * nhp + h))

    def cw(group):
        return pl.BlockSpec((CONV_W, width), lambda bi, h, si: (0, group * nhp + h))

    ab = pl.BlockSpec((1, hps, s // GROUP, GROUP), lambda bi, h, si: (bi, h, 0, 0))
    return pl.pallas_call(
        functools.partial(_gdn_kernel, ts=ts, hps=hps),
        grid=(b, nhp, s // ts),
        in_specs=[smem, smem, col(0), col(1), col(2), col(3), ab, ab,
                  cw(0), cw(1), cw(2),
                  pl.BlockSpec((1, HEAD_DIM), lambda bi, h, si: (0, 0))],
        out_specs=pl.BlockSpec((1, ts, width), lambda bi, h, si: (bi, si, h)),
        out_shape=jax.ShapeDtypeStruct((b, s, D_GDN), BF16),
        scratch_shapes=[pltpu.VMEM((3, ts + 8, width), F32),
                        pltpu.VMEM((hps, HEAD_DIM, HEAD_DIM), F32),
                        pltpu.VMEM((5, GROUP, GROUP), F32)],
        compiler_params=_cparams(("parallel", "parallel", "arbitrary")),
        name="gdn",
    )(a_log, dt_bias, proj_g, proj_g, proj_g, proj_g, a_t, b_t,
      conv_w, conv_w, conv_w, gain)


def _pick(n, pref):
    t = min(pref, n)
    while n % t:
        t -= 128
    return t


def _ffn_weights(wg, wu, wd):
    return wg.astype(BF16), wu.astype(BF16), wd.astype(BF16)


def kernel(x, ffn1_norm, ffn1_w_gate, ffn1_w_up, ffn1_w_down, mix_norm, w_in, sb_out_norm, conv_w, a_log, dt_bias, gdn_out_norm, w_out, ffn2_norm, ffn2_w_gate, ffn2_w_up, ffn2_w_down, final_norm):
    bsz, s_len, d = x.shape
    t = bsz * s_len
    assert ffn1_norm.shape[0] == 1, "single-layer block"
    l = 0
    tm = _pick(t, 512)
    tf = 512
    x2d = x.reshape(t, d)
    row = lambda v: v.reshape(1, -1)

    wg, wu, wd = _ffn_weights(ffn1_w_gate[l], ffn1_w_up[l], ffn1_w_down[l])
    x1, hmix = _ffn(x2d, row(ffn1_norm[l]), wg, wu, wd, row(mix_norm[l]), "mix", tm, tf)

    w = w_in[l]
    o_qkv = 3 * D_SB
    o_a = o_qkv + 3 * D_GDN
    o_gate = o_a + 2 * GDN_HEADS
    w_sb = w[:, :o_qkv].astype(BF16)
    w_g = jnp.concatenate([w[:, o_qkv:o_a], w[:, o_gate:]], axis=1).astype(BF16)
    w_ab = jnp.pad(w[:, o_a:o_gate], ((0, 0), (0, HEAD_DIM - 2 * GDN_HEADS))).astype(BF16)
    tmm = _pick(t, 1024)
    proj_sb = _matmul(hmix, w_sb, BF16, tmm, 1024).reshape(bsz, s_len, 3 * D_SB)
    proj_g = _matmul(hmix, w_g, F32, tmm, 1024).reshape(bsz, s_len, 4 * D_GDN)
    proj_ab = _matmul(hmix, w_ab, F32, tmm, HEAD_DIM).reshape(bsz, s_len, HEAD_DIM)

    def per_head(cols):
        return cols.transpose(0, 2, 1).reshape(bsz, GDN_HEADS, s_len // GROUP, GROUP)

    a_t = per_head(proj_ab[..., :GDN_HEADS])
    b_t = per_head(proj_ab[..., GDN_HEADS:2 * GDN_HEADS])

    tq = _pick(s_len, 512)
    o_sb = _sb_attention(proj_sb, row(sb_out_norm[l]), tq, _pick(tq, 256))
    o_g = _gdn(proj_g, a_t, b_t, conv_w[l], a_log[l], dt_bias[l],
               row(gdn_out_norm[l]), _pick(s_len, 512), 4)

    x2 = _out_proj(o_sb.reshape(t, D_SB), o_g.reshape(t, D_GDN),
                   w_out[l].astype(BF16), x1, tmm, _pick(d, 1024))

    wg, wu, wd = _ffn_weights(ffn2_w_gate[l], ffn2_w_up[l], ffn2_w_down[l])
    (out,) = _ffn(x2, row(ffn2_norm[l]), wg, wu, wd, row(final_norm), "final", tm, tf)
    return out.reshape(bsz, s_len, d)
```

```python
import functools
import math

import jax
import jax.numpy as jnp
from jax import lax
from jax.experimental import pallas as pl
from jax.experimental.pallas import tpu as pltpu

F32 = jnp.float32
BF16 = jnp.bfloat16
EPS = 1e-6

SB_HEADS = 8
GDN_HEADS = 8
HEAD_DIM = 128
CHUNK = 64
GROUP = 256
CONV_W = 4
D_SB = SB_HEADS * HEAD_DIM
D_GDN = GDN_HEADS * HEAD_DIM

VMEM_LIMIT_BYTES = 56 * 1024 * 1024
MASKED_LOG = -1e30


def _cparams(semantics):
    return pltpu.CompilerParams(dimension_semantics=semantics,
                                vmem_limit_bytes=VMEM_LIMIT_BYTES)


def _rms(x, gain):
    return x * lax.rsqrt(jnp.mean(x * x, axis=-1, keepdims=True) + EPS) * gain


def _sigmoid(x):
    return 1.0 / (1.0 + jnp.exp(-x))


def _silu(x):
    return x * _sigmoid(x)


def _dot(a, b):
    return jnp.dot(a, b, preferred_element_type=F32)


def _dot_nt(a, b):
    return lax.dot_general(a, b, (((1,), (1,)), ((), ())), preferred_element_type=F32)


def _split(x):
    hi = x.astype(BF16)
    lo = (x - hi.astype(F32)).astype(BF16)
    return hi, lo


def _ffn_kernel(x_ref, g_ref, wg_ref, wu_ref, wd_ref, wgt_ref, wut_ref, wdt_ref, g2_ref,
                *refs, n_main, post):
    if post == "mix":
        o_ref, h_ref, xn_ref, acc_ref = refs
    else:
        o_ref, xn_ref, acc_ref = refs
    f = pl.program_id(1)

    def accumulate(wg, wu, wd):
        xn = xn_ref[...]
        hid = (_silu(_dot(xn, wg)) * _dot(xn, wu)).astype(BF16)
        acc_ref[...] += _dot(hid, wd)

    @pl.when(f == 0)
    def _():
        xn_ref[...] = _rms(x_ref[...], g_ref[...]).astype(BF16)
        acc_ref[...] = jnp.zeros_like(acc_ref)

    @pl.when(f < n_main)
    def _():
        accumulate(wg_ref[...], wu_ref[...], wd_ref[...])

    @pl.when(f == n_main)
    def _():
        accumulate(wgt_ref[...], wut_ref[...], wdt_ref[...])
        y = x_ref[...] + 0.5 * acc_ref[...]
        if post == "mix":
            o_ref[...] = y
            h_ref[...] = _rms(y, g2_ref[...]).astype(BF16)
        else:
            o_ref[...] = _rms(y, g2_ref[...])


def _ffn(x, gain, wg, wu, wd, gain2, post, tm, tf):
    t, d = x.shape
    f_dim = wg.shape[1]
    n_main = (f_dim - 1) // tf
    cut = n_main * tf
    tail = f_dim - cut
    last = n_main - 1
    out_shape = [jax.ShapeDtypeStruct((t, d), F32)]
    out_specs = [pl.BlockSpec((tm, d), lambda i, f: (i, 0))]
    if post == "mix":
        out_shape.append(jax.ShapeDtypeStruct((t, d), BF16))
        out_specs.append(pl.BlockSpec((tm, d), lambda i, f: (i, 0)))
    res = pl.pallas_call(
        functools.partial(_ffn_kernel, n_main=n_main, post=post),
        grid=(t // tm, n_main + 1),
        in_specs=[
            pl.BlockSpec((tm, d), lambda i, f: (i, 0)),
            pl.BlockSpec((1, d), lambda i, f: (0, 0)),
            pl.BlockSpec((d, tf), lambda i, f: (0, jnp.minimum(f, last))),
            pl.BlockSpec((d, tf), lambda i, f: (0, jnp.minimum(f, last))),
            pl.BlockSpec((tf, d), lambda i, f: (jnp.minimum(f, last), 0)),
            pl.BlockSpec((d, tail), lambda i, f: (0, 0)),
            pl.BlockSpec((d, tail), lambda i, f: (0, 0)),
            pl.BlockSpec((tail, d), lambda i, f: (0, 0)),
            pl.BlockSpec((1, d), lambda i, f: (0, 0)),
        ],
        out_specs=out_specs,
        out_shape=out_shape,
        scratch_shapes=[pltpu.VMEM((tm, d), BF16), pltpu.VMEM((tm, d), F32)],
        compiler_params=_cparams(("parallel", "arbitrary")),
        name="ffn_" + post,
    )(x, gain, wg, wu, wd, wg[:, cut:], wu[:, cut:], wd[cut:], gain2)
    return res


def _mm_kernel(x_ref, w_ref, o_ref):
    o_ref[...] = _dot(x_ref[...], w_ref[...]).astype(o_ref.dtype)


def _matmul(x, w, out_dtype, tm, tn):
    m, k = x.shape
    n = w.shape[1]
    return pl.pallas_call(
        _mm_kernel,
        grid=(m // tm, n // tn),
        in_specs=[pl.BlockSpec((tm, k), lambda i, j: (i, 0)),
                  pl.BlockSpec((k, tn), lambda i, j: (0, j))],
        out_specs=pl.BlockSpec((tm, tn), lambda i, j: (i, j)),
        out_shape=jax.ShapeDtypeStruct((m, n), out_dtype),
        compiler_params=_cparams(("parallel", "parallel")),
        name="in_proj_n%d" % n,
    )(x, w)


def _mm_side_kernel(x_ref, w_ref, ws_ref, o_ref, os_ref):
    o_ref[...] = _dot(x_ref[...], w_ref[...]).astype(o_ref.dtype)

    @pl.when(pl.program_id(1) == 0)
    def _():
        os_ref[...] = _dot(x_ref[...], ws_ref[...]).astype(os_ref.dtype)


def _matmul_with_side(x, w, w_side, tm, tn):
    m, k = x.shape
    n = w.shape[1]
    ns = w_side.shape[1]
    return pl.pallas_call(
        _mm_side_kernel,
        grid=(m // tm, n // tn),
        in_specs=[pl.BlockSpec((tm, k), lambda i, j: (i, 0)),
                  pl.BlockSpec((k, tn), lambda i, j: (0, j)),
                  pl.BlockSpec((k, ns), lambda i, j: (0, 0))],
        out_specs=[pl.BlockSpec((tm, tn), lambda i, j: (i, j)),
                   pl.BlockSpec((tm, ns), lambda i, j: (i, 0))],
        out_shape=[jax.ShapeDtypeStruct((m, n), F32), jax.ShapeDtypeStruct((m, ns), F32)],
        compiler_params=_cparams(("parallel", "arbitrary")),
        name="in_proj_n%d_side" % n,
    )(x, w, w_side)


def _out_proj_kernel(ya_ref, yb_ref, wa_ref, wb_ref, r_ref, o_ref):
    o_ref[...] = (r_ref[...] + _dot(ya_ref[...], wa_ref[...])
                  + _dot(yb_ref[...], wb_ref[...]))


def _out_proj(ya, yb, w, resid, tm, tn):
    m, ka = ya.shape
    kb = yb.shape[1]
    n = w.shape[1]
    assert ka == kb
    return pl.pallas_call(
        _out_proj_kernel,
        grid=(m // tm, n // tn),
        in_specs=[pl.BlockSpec((tm, ka), lambda i, j: (i, 0)),
                  pl.BlockSpec((tm, kb), lambda i, j: (i, 0)),
                  pl.BlockSpec((ka, tn), lambda i, j: (0, j)),
                  pl.BlockSpec((kb, tn), lambda i, j: (1, j)),
                  pl.BlockSpec((tm, tn), lambda i, j: (i, j))],
        out_specs=pl.BlockSpec((tm, tn), lambda i, j: (i, j)),
        out_shape=jax.ShapeDtypeStruct((m, n), F32),
        compiler_params=_cparams(("parallel", "parallel")),
        name="out_proj",
    )(ya, yb, w, w, resid)


def _sb_kernel(q_ref, k_ref, v_ref, gain_ref, o_ref,
               acc_ref, run_ref, lb_ref, hi_ref, lo_ref, rs_ref, *, tq, tk, scale):
    assert tq == 2 * tk
    i = pl.program_id(2)
    sub = HEAD_DIM
    nsub = tq // sub
    row = lax.broadcasted_iota(jnp.int32, (sub, tk), 0)
    col = lax.broadcasted_iota(jnp.int32, (sub, tk), 1)
    r2 = lax.broadcasted_iota(jnp.int32, (tk, tk), 0)
    c2 = lax.broadcasted_iota(jnp.int32, (tk, tk), 1)
    later = (r2 > c2).astype(BF16)
    acc_ref[...] = jnp.zeros_like(acc_ref)
    run_ref[...] = jnp.zeros_like(run_ref)

    def rows(r):
        return slice(r * sub, (r + 1) * sub)

    def mask_kind(diag_off, r):
        if diag_off is None or diag_off + tk - 1 < r * sub:
            return "all"
        if diag_off >= (r + 1) * sub - 1:
            return "none"
        return "part"

    def key_rows(j):
        return pl.ds(pl.multiple_of(j * tk, tk), tk)

    def matmuls(weigh_blk, score_blk, r):
        z = sfx = None
        if score_blk is not None and mask_kind(score_blk[2], r) != "none":
            z = _dot_nt(q_ref[0, rows(r), :], k_ref[0, key_rows(score_blk[0]), :])
        if weigh_blk is not None and mask_kind(weigh_blk[2], r) != "none":
            slot_w = weigh_blk[1]
            sfx = (_dot(hi_ref[slot_w, rows(r), :], later)
                   + _dot(lo_ref[slot_w, rows(r), :], later))
        return z, sfx

    def elementwise(weigh_blk, score_blk, r, z, sfx):
        if score_blk is not None and mask_kind(score_blk[2], r) != "none":
            _, slot_s, off_s = score_blk
            z = z * scale
            log_beta = jnp.minimum(z, 0.0) - jnp.log(1.0 + jnp.exp(-jnp.abs(z)))
            log_keep = log_beta - z
            if mask_kind(off_s, r) == "part":
                mask = (col + off_s) < (row + r * sub)
                log_keep = jnp.where(mask, log_keep, 0.0)
                log_beta = jnp.where(mask, log_beta, MASKED_LOG)
            hi, lo = _split(log_keep)
            lb_ref[slot_s, rows(r), :] = log_beta
            hi_ref[slot_s, rows(r), :] = hi
            lo_ref[slot_s, rows(r), :] = lo
            rs_ref[slot_s, rows(r), :] = jnp.broadcast_to(
                jnp.sum(log_keep, axis=-1, keepdims=True), (sub, HEAD_DIM))
        if weigh_blk is not None and mask_kind(weigh_blk[2], r) != "none":
            jw, slot_w, _ = weigh_blk
            run = jnp.concatenate([run_ref[rows(r), :]] * (tk // HEAD_DIM), axis=1)
            w = jnp.exp(lb_ref[slot_w, rows(r), :] + sfx + run)
            acc_ref[rows(r), :] += _dot(w.astype(BF16), v_ref[0, key_rows(jw), :])
            run_ref[rows(r), :] += rs_ref[slot_w, rows(r), :]

    def step(first, weigh_blk, score_blk, nxt):
        cur = matmuls(weigh_blk, score_blk, 0) if first is None else first
        for r in range(nsub):
            if r + 1 < nsub:
                ahead = matmuls(weigh_blk, score_blk, r + 1)
            else:
                ahead = matmuls(nxt[0], nxt[1], 0) if nxt is not None else None
            elementwise(weigh_blk, score_blk, r, *cur)
            cur = ahead
        return cur

    step(None, None, (2 * i + 1, 0, tk), None)
    first = step(None, (2 * i + 1, 0, tk), (2 * i, 1, 0),
                 ((2 * i, 1, None), (jnp.maximum(2 * i - 1, 0), 0, None)))

    def body(p, first):
        j = 2 * (i - p)
        first = step(first, (j, 1, None), (j - 1, 0, None),
                     ((j - 1, 0, None), (j - 2, 1, None)))
        return step(first, (j - 1, 0, None), (j - 2, 1, None),
                    ((j - 2, 1, None), (jnp.maximum(j - 3, 0), 0, None)))

    first = lax.fori_loop(0, i, body, first)
    step(first, (0, 1, None), None, None)
    o_ref[0] = _rms(acc_ref[...], gain_ref[...]).astype(o_ref.dtype)


def _sb_attention(proj_sb, gain, tq, tk):
    b, s, _ = proj_sb.shape
    nh = SB_HEADS
    return pl.pallas_call(
        functools.partial(_sb_kernel, tq=tq, tk=tk, scale=1.0 / math.sqrt(HEAD_DIM)),
        grid=(b, nh, s // tq),
        in_specs=[
            pl.BlockSpec((1, tq, HEAD_DIM), lambda bi, h, i: (bi, i, h)),
            pl.BlockSpec((1, s, HEAD_DIM), lambda bi, h, i: (bi, 0, nh + h)),
            pl.BlockSpec((1, s, HEAD_DIM), lambda bi, h, i: (bi, 0, 2 * nh + h)),
            pl.BlockSpec((1, HEAD_DIM), lambda bi, h, i: (0, 0)),
        ],
        out_specs=pl.BlockSpec((1, tq, HEAD_DIM), lambda bi, h, i: (bi, i, h)),
        out_shape=jax.ShapeDtypeStruct((b, s, D_SB), BF16),
        scratch_shapes=[pltpu.VMEM((tq, HEAD_DIM), F32), pltpu.VMEM((tq, HEAD_DIM), F32),
                        pltpu.VMEM((2, tq, tk), F32), pltpu.VMEM((2, tq, tk), BF16),
                        pltpu.VMEM((2, tq, tk), BF16), pltpu.VMEM((2, tq, HEAD_DIM), F32)],
        compiler_params=_cparams(("parallel", "parallel", "arbitrary")),
        name="sb_attn",
    )(proj_sb, proj_sb, proj_sb, gain)


def _dot3(a_hi, a_lo, b_hi, b_lo):
    return _dot(a_hi, b_hi) + (_dot(a_lo, b_hi) + _dot(a_hi, b_lo))


def _unit_lower_inverse(a_bds, same_chunk):
    n = GROUP // CHUNK
    r = lax.broadcasted_iota(jnp.int32, (CHUNK, GROUP), 0)
    cl = lax.broadcasted_iota(jnp.int32, (CHUNK, GROUP), 1) & (CHUNK - 1)
    same16 = (r >> 4) == (cl >> 4)
    same32 = (r >> 5) == (cl >> 5)
    eye_p = (r == cl).astype(F32)

    def packed(a_bd):
        a_p = a_bd[0:CHUNK]
        for c in range(1, n):
            a_p = a_p + a_bd[c * CHUNK:(c + 1) * CHUNK]
        return a_p

    def block_diag(x_p):
        return jnp.concatenate([x_p] * n, axis=0) * same_chunk().astype(x_p.dtype)

    def mm(xs, ys):
        out = []
        for x_p, y_p in zip(xs, ys):
            xh, xl = _split(x_p)
            yh, yl = _split(y_p)
            out.append(_dot3(xh, xl, block_diag(yh), block_diag(yl)))
        return out

    def mm1(xs, ys):
        return [_dot(x_p.astype(BF16), block_diag(y_p.astype(BF16))) for x_p, y_p in zip(xs, ys)]

    def axpy(xs, ys, sign):
        return [x + y if sign > 0 else x - y for x, y in zip(xs, ys)]

    a_ps = [packed(a) for a in a_bds]
    a16 = [jnp.where(same16, a, 0.0) for a in a_ps]
    x = [eye_p - a for a in a16]
    p = mm1(a16, a16)
    x = axpy(x, mm1(x, p), 1.0)
    p = mm1(p, p)
    x = axpy(x, mm1(x, p), 1.0)
    p = mm1(p, p)
    x = axpy(x, mm1(x, p), 1.0)
    e32 = [jnp.where(same32 & ~same16, a, 0.0) for a in a_ps]
    x = axpy(x, mm(x, mm(e32, x)), -1.0)
    e64 = [jnp.where(same32, 0.0, a) for a in a_ps]
    x = axpy(x, mm(x, mm(e64, x)), -1.0)
    return [block_diag(xi) for xi in x]


M_SAME, M_LOWER, M_UPPER, M_EYE, M_STRICT = range(5)


def _gdn_masks():
    shape = (GROUP, GROUP)
    r = lax.broadcasted_iota(jnp.int32, shape, 0)
    c = lax.broadcasted_iota(jnp.int32, shape, 1)
    same = (r >> 6) == (c >> 6)
    return [m.astype(F32) for m in
            (same, same & (c <= r), same & (r <= c), r == c, same & (c < r))]


def _gdn_groups(qs, ks, vs, a_rows, b_rows, neg_rates, dt_biases, states, mask_ref):
    n = GROUP // CHUNK
    ng = len(qs)
    steps = (ng // len(states)) * n
    shape = (GROUP, GROUP)

    def mask(m):
        return mask_ref[m]

    def rowsum(x):
        return jnp.sum(x, axis=1, keepdims=True)

    def colsum(x):
        return jnp.sum(x, axis=0, keepdims=True)

    decay, exp_gc, kend_scale, glast_rows, beta_cols = [], [], [], [], []
    for a_row, b_row, neg_rate, dt_bias in zip(a_rows, b_rows, neg_rates, dt_biases):
        sp_in = a_row + dt_bias
        g_row = neg_rate * (jnp.maximum(sp_in, 0.0) + jnp.log(1.0 + jnp.exp(-jnp.abs(sp_in))))
        g_b = jnp.broadcast_to(g_row, shape)
        gc_col = rowsum(g_b * mask(M_LOWER))
        glast_col = rowsum(g_b * mask(M_SAME))
        g_col = rowsum(g_b * mask(M_EYE))
        beta_cols.append(rowsum(jnp.broadcast_to(_sigmoid(b_row), shape) * mask(M_EYE)))
        gcol_b = jnp.broadcast_to(g_col, shape)
        gc_row = colsum(gcol_b * mask(M_UPPER))
        glast_rows.append(colsum(gcol_b * mask(M_SAME)))
        decay.append(jnp.exp(jnp.where(mask(M_LOWER) > 0.5, gc_col - gc_row, MASKED_LOG)))
        exp_gc.append(jnp.exp(gc_col))
        kend_scale.append(jnp.exp(glast_col - gc_col))

    kbs = [k * b for k, b in zip(ks, beta_cols)]
    grams = [_dot_nt(jnp.concatenate([kb, q], axis=0).astype(BF16), k.astype(BF16))
             for kb, q, k in zip(kbs, qs, ks)]
    a_bds = [g[:GROUP] * d * mask(M_STRICT) for g, d in zip(grams, decay)]
    aqk_bds = [(g[GROUP:] * d).astype(BF16) for g, d in zip(grams, decay)]

    tinvs = _unit_lower_inverse(a_bds, functools.partial(mask, M_SAME))
    wus = []
    for tinv, kb, v, eg, bc in zip(tinvs, kbs, vs, exp_gc, beta_cols):
        th, tl = _split(tinv)
        rh, rl = _split(jnp.concatenate([kb * eg, v * bc], axis=1))
        wus.append(_dot3(th, tl, rh, rl))

    lane_chunk = lax.broadcasted_iota(jnp.int32, (HEAD_DIM, GROUP), 1) >> 6
    lane = lax.broadcasted_iota(jnp.int32, (1, GROUP), 1)
    pns, decs = [], []
    for k, scale, wu, gl in zip(ks, kend_scale, wus, glast_rows):
        kend_t = (k * scale).T
        wu_b = wu.astype(BF16)
        for ci in range(n):
            pns.append(_dot(jnp.where(lane_chunk == ci, kend_t, 0.0).astype(BF16), wu_b))
            decs.append(jnp.exp(jnp.sum(jnp.where(lane == ci * CHUNK, gl, 0.0),
                                        axis=1, keepdims=True)))

    states = list(states)
    chunk_states = [None] * len(pns)
    for t in range(steps):
        for hh in range(len(states)):
            idx = hh * steps + t
            pn = pns[idx]
            chunk_states[idx] = states[hh].astype(BF16)
            states[hh] = ((decs[idx] * states[hh] + pn[:, HEAD_DIM:])
                          - _dot(pn[:, :HEAD_DIM].astype(BF16), chunk_states[idx]))

    outs = []
    for gi in range(ng):
        w4 = wus[gi][:, :HEAD_DIM]
        u4 = wus[gi][:, HEAD_DIM:]
        qdec4 = qs[gi] * exp_gc[gi]
        v_new, o_q = [], []
        for ci in range(n):
            lo = ci * CHUNK
            lhs = jnp.concatenate([w4[lo:lo + CHUNK], qdec4[lo:lo + CHUNK]], axis=0).astype(BF16)
            res = _dot(lhs, chunk_states[gi * n + ci])
            v_new.append(u4[lo:lo + CHUNK] - res[:CHUNK])
            o_q.append(res[CHUNK:])
        v_new = jnp.concatenate(v_new, axis=0).astype(BF16)
        outs.append(jnp.concatenate(o_q, axis=0) + _dot(aqk_bds[gi], v_new))
    return outs, states


def _gdn_kernel(alog_ref, dtb_ref, q_ref, k_ref, v_ref, gate_ref, a_ref, b_ref,
                cwq_ref, cwk_ref, cwv_ref, gain_ref, o_ref,
                ext_ref, state_ref, mask_ref, *, ts, hps):
    hp = pl.program_id(1)
    s = pl.program_id(2)
    ng = ts // GROUP
    pad = 8

    @pl.when(s == 0)
    def _():
        ext_ref[:, 0:pad, :] = jnp.zeros((3, pad, hps * HEAD_DIM), F32)
        state_ref[...] = jnp.zeros_like(state_ref)
        for m, mask in enumerate(_gdn_masks()):
            mask_ref[m] = mask

    def conv_silu(idx, x_ref, cw_ref):
        ext_ref[idx, pad:pad + ts, :] = x_ref[0]
        acc = None
        for tap in range(CONV_W):
            off = pad - (CONV_W - 1) + tap
            term = cw_ref[tap:tap + 1, :] * ext_ref[idx, off:off + ts, :]
            acc = term if acc is None else acc + term
        ext_ref[idx, 0:pad, :] = ext_ref[idx, ts:ts + pad, :]
        return _silu(acc)

    ones = jnp.ones((HEAD_DIM, HEAD_DIM), BF16)

    def l2n(x):
        hi, lo = _split(x * x)
        sumsq = _dot(hi, ones) + _dot(lo, ones)
        return x * lax.rsqrt(sumsq + EPS)

    def groups(x, hh, fn=None):
        out = []
        for gi in range(ng):
            blk = x[gi * GROUP:(gi + 1) * GROUP, hh * HEAD_DIM:(hh + 1) * HEAD_DIM]
            out.append(blk if fn is None else fn(blk))
        return out

    q_all = conv_silu(0, q_ref, cwq_ref)
    k_all = conv_silu(1, k_ref, cwk_ref)
    v_all = conv_silu(2, v_ref, cwv_ref)
    qs, ks, vs, a_rows, b_rows, neg_rates, dt_biases = [], [], [], [], [], [], []
    for hh in range(hps):
        qs += groups(q_all, hh, lambda x: l2n(x) * (1.0 / math.sqrt(HEAD_DIM)))
        ks += groups(k_all, hh, l2n)
        vs += groups(v_all, hh)
        a_rows += [a_ref[0, hh, pl.ds(s * ng + gi, 1), :] for gi in range(ng)]
        b_rows += [b_ref[0, hh, pl.ds(s * ng + gi, 1), :] for gi in range(ng)]
        neg_rates += [-jnp.exp(jnp.full((1, GROUP), alog_ref[hp * hps + hh], F32))] * ng
        dt_biases += [jnp.full((1, GROUP), dtb_ref[hp * hps + hh], F32)] * ng

    outs, states = _gdn_groups(qs, ks, vs, a_rows, b_rows, neg_rates, dt_biases,
                               [state_ref[hh] for hh in range(hps)], mask_ref)
    gain = gain_ref[...]
    for hh in range(hps):
        state_ref[hh] = states[hh]
        lanes = slice(hh * HEAD_DIM, (hh + 1) * HEAD_DIM)
        for gi in range(ng):
            lo = gi * GROUP
            gate = gate_ref[0, lo:lo + GROUP, lanes]
            o_ref[0, lo:lo + GROUP, lanes] = (_rms(outs[hh * ng + gi], gain)
                                              * _silu(gate)).astype(o_ref.dtype)


def _gdn(proj_g, a_t, b_t, conv_w, a_log, dt_bias, gain, ts, hps):
    b, s, _ = proj_g.shape
    nhp = GDN_HEADS // hps
    width = hps * HEAD_DIM
    smem = pl.BlockSpec(memory_space=pltpu.SMEM)

    def col(group):
        return pl.BlockSpec((1, ts, width), lambda bi, h, si: (bi, si, group * nhp + h))

    def cw(group):
        return pl.BlockSpec((CONV_W, width), lambda bi, h, si: (0, group * nhp + h))

    ab = pl.BlockSpec((1, hps, s // GROUP, GROUP), lambda bi, h, si: (bi, h, 0, 0))
    return pl.pallas_call(
        functools.partial(_gdn_kernel, ts=ts, hps=hps),
        grid=(b, nhp, s // ts),
        in_specs=[smem, smem, col(0), col(1), col(2), col(3), ab, ab,
                  cw(0), cw(1), cw(2),
                  pl.BlockSpec((1, HEAD_DIM), lambda bi, h, si: (0, 0))],
        out_specs=pl.BlockSpec((1, ts, width), lambda bi, h, si: (bi, si, h)),
        out_shape=jax.ShapeDtypeStruct((b, s, D_GDN), BF16),
        scratch_shapes=[pltpu.VMEM((3, ts + 8, width), F32),
                        pltpu.VMEM((hps, HEAD_DIM, HEAD_DIM), F32),
                        pltpu.VMEM((5, GROUP, GROUP), F32)],
        compiler_params=_cparams(("parallel", "parallel", "arbitrary")),
        name="gdn",
    )(a_log, dt_bias, proj_g, proj_g, proj_g, proj_g, a_t, b_t,
      conv_w, conv_w, conv_w, gain)


def _pick(n, pref):
    t = min(pref, n)
    while n % t:
        t -= 128
    return t


def _ffn_weights(wg, wu, wd):
    return wg.astype(BF16), wu.astype(BF16), wd.astype(BF16)


def kernel(x, ffn1_norm, ffn1_w_gate, ffn1_w_up, ffn1_w_down, mix_norm, w_in, sb_out_norm, conv_w, a_log, dt_bias, gdn_out_norm, w_out, ffn2_norm, ffn2_w_gate, ffn2_w_up, ffn2_w_down, final_norm):
    bsz, s_len, d = x.shape
    t = bsz * s_len
    assert ffn1_norm.shape[0] == 1, "single-layer block"
    l = 0
    tm = _pick(t, 512)
    tf = 512
    x2d = x.reshape(t, d)
    row = lambda v: v.reshape(1, -1)

    wg, wu, wd = _ffn_weights(ffn1_w_gate[l], ffn1_w_up[l], ffn1_w_down[l])
    x1, hmix = _ffn(x2d, row(ffn1_norm[l]), wg, wu, wd, row(mix_norm[l]), "mix", tm, tf)

    w = w_in[l]
    o_qkv = 3 * D_SB
    o_a = o_qkv + 3 * D_GDN
    o_gate = o_a + 2 * GDN_HEADS
    w_sb = w[:, :o_qkv].astype(BF16)
    w_g = jnp.concatenate([w[:, o_qkv:o_a], w[:, o_gate:]], axis=1).astype(BF16)
    w_ab = jnp.pad(w[:, o_a:o_gate], ((0, 0), (0, HEAD_DIM - 2 * GDN_HEADS))).astype(BF16)
    tmm = _pick(t, 1024)
    proj_sb = _matmul(hmix, w_sb, BF16, tmm, 1024).reshape(bsz, s_len, 3 * D_SB)
    proj_g, proj_ab = _matmul_with_side(hmix, w_g, w_ab, tmm, 1024)
    proj_g = proj_g.reshape(bsz, s_len, 4 * D_GDN)
    proj_ab = proj_ab.reshape(bsz, s_len, HEAD_DIM)

    def per_head(cols):
        return cols.transpose(0, 2, 1).reshape(bsz, GDN_HEADS, s_len // GROUP, GROUP)

    a_t = per_head(proj_ab[..., :GDN_HEADS])
    b_t = per_head(proj_ab[..., GDN_HEADS:2 * GDN_HEADS])

    tq = _pick(s_len, 512)
    o_sb = _sb_attention(proj_sb, row(sb_out_norm[l]), tq, _pick(tq, 256))
    o_g = _gdn(proj_g, a_t, b_t, conv_w[l], a_log[l], dt_bias[l],
               row(gdn_out_norm[l]), _pick(s_len, 512), 4)

    x2 = _out_proj(o_sb.reshape(t, D_SB), o_g.reshape(t, D_GDN),
                   w_out[l].astype(BF16), x1, tmm, _pick(d, 1024))

    wg, wu, wd = _ffn_weights(ffn2_w_gate[l], ffn2_w_up[l], ffn2_w_down[l])
    (out,) = _ffn(x2, row(ffn2_norm[l]), wg, wu, wd, row(final_norm), "final", tm, tf)
    return out.reshape(bsz, s_len, d)
```

```python
import functools
import math

import jax
import jax.numpy as jnp
from jax import lax
from jax.experimental import pallas as pl
from jax.experimental.pallas import tpu as pltpu

F32 = jnp.float32
BF16 = jnp.bfloat16
EPS = 1e-6

SB_HEADS = 8
GDN_HEADS = 8
HEAD_DIM = 128
CHUNK = 64
GROUP = 256
CONV_W = 4
D_SB = SB_HEADS * HEAD_DIM
D_GDN = GDN_HEADS * HEAD_DIM

VMEM_LIMIT_BYTES = 56 * 1024 * 1024
MASKED_LOG = -1e30


def _cparams(semantics):
    return pltpu.CompilerParams(dimension_semantics=semantics,
                                vmem_limit_bytes=VMEM_LIMIT_BYTES)


def _rms(x, gain):
    return x * lax.rsqrt(jnp.mean(x * x, axis=-1, keepdims=True) + EPS) * gain


def _sigmoid(x):
    return 1.0 / (1.0 + jnp.exp(-x))


def _silu(x):
    return x * _sigmoid(x)


def _dot(a, b):
    return jnp.dot(a, b, preferred_element_type=F32)


def _dot_nt(a, b):
    return lax.dot_general(a, b, (((1,), (1,)), ((), ())), preferred_element_type=F32)


def _split(x):
    hi = x.astype(BF16)
    lo = (x - hi.astype(F32)).astype(BF16)
    return hi, lo


def _ffn_kernel(x_ref, g_ref, wg_ref, wu_ref, wd_ref, wgt_ref, wut_ref, wdt_ref, g2_ref,
                *refs, n_main, post):
    if post == "mix":
        o_ref, h_ref, xn_ref, acc_ref = refs
    else:
        o_ref, xn_ref, acc_ref = refs
    f = pl.program_id(1)

    def accumulate(wg, wu, wd):
        xn = xn_ref[...]
        hid = (_silu(_dot(xn, wg)) * _dot(xn, wu)).astype(BF16)
        acc_ref[...] += _dot(hid, wd)

    @pl.when(f == 0)
    def _():
        xn_ref[...] = _rms(x_ref[...], g_ref[...]).astype(BF16)
        acc_ref[...] = jnp.zeros_like(acc_ref)

    @pl.when(f < n_main)
    def _():
        accumulate(wg_ref[...], wu_ref[...], wd_ref[...])

    @pl.when(f == n_main)
    def _():
        accumulate(wgt_ref[...], wut_ref[...], wdt_ref[...])
        y = x_ref[...] + 0.5 * acc_ref[...]
        if post == "mix":
            o_ref[...] = y
            h_ref[...] = _rms(y, g2_ref[...]).astype(BF16)
        else:
            o_ref[...] = _rms(y, g2_ref[...])


def _ffn(x, gain, wg, wu, wd, gain2, post, tm, tf):
    t, d = x.shape
    f_dim = wg.shape[1]
    n_main = (f_dim - 1) // tf
    cut = n_main * tf
    tail = f_dim - cut
    last = n_main - 1
    out_shape = [jax.ShapeDtypeStruct((t, d), F32)]
    out_specs = [pl.BlockSpec((tm, d), lambda i, f: (i, 0))]
    if post == "mix":
        out_shape.append(jax.ShapeDtypeStruct((t, d), BF16))
        out_specs.append(pl.BlockSpec((tm, d), lambda i, f: (i, 0)))
    res = pl.pallas_call(
        functools.partial(_ffn_kernel, n_main=n_main, post=post),
        grid=(t // tm, n_main + 1),
        in_specs=[
            pl.BlockSpec((tm, d), lambda i, f: (i, 0)),
            pl.BlockSpec((1, d), lambda i, f: (0, 0)),
            pl.BlockSpec((d, tf), lambda i, f: (0, jnp.minimum(f, last))),
            pl.BlockSpec((d, tf), lambda i, f: (0, jnp.minimum(f, last))),
            pl.BlockSpec((tf, d), lambda i, f: (jnp.minimum(f, last), 0)),
            pl.BlockSpec((d, tail), lambda i, f: (0, 0)),
            pl.BlockSpec((d, tail), lambda i, f: (0, 0)),
            pl.BlockSpec((tail, d), lambda i, f: (0, 0)),
            pl.BlockSpec((1, d), lambda i, f: (0, 0)),
        ],
        out_specs=out_specs,
        out_shape=out_shape,
        scratch_shapes=[pltpu.VMEM((tm, d), BF16), pltpu.VMEM((tm, d), F32)],
        compiler_params=_cparams(("parallel", "arbitrary")),
        name="ffn_" + post,
    )(x, gain, wg, wu, wd, wg[:, cut:], wu[:, cut:], wd[cut:], gain2)
    return res


def _mm_kernel(x_ref, w_ref, o_ref):
    o_ref[...] = _dot(x_ref[...], w_ref[...]).astype(o_ref.dtype)


def _matmul(x, w, out_dtype, tm, tn):
    m, k = x.shape
    n = w.shape[1]
    return pl.pallas_call(
        _mm_kernel,
        grid=(m // tm, n // tn),
        in_specs=[pl.BlockSpec((tm, k), lambda i, j: (i, 0)),
                  pl.BlockSpec((k, tn), lambda i, j: (0, j))],
        out_specs=pl.BlockSpec((tm, tn), lambda i, j: (i, j)),
        out_shape=jax.ShapeDtypeStruct((m, n), out_dtype),
        compiler_params=_cparams(("parallel", "parallel")),
        name="in_proj_n%d" % n,
    )(x, w)


def _mm_side_kernel(x_ref, w_ref, ws_ref, o_ref, os_ref):
    o_ref[...] = _dot(x_ref[...], w_ref[...]).astype(o_ref.dtype)

    @pl.when(pl.program_id(1) == 0)
    def _():
        os_ref[...] = _dot(x_ref[...], ws_ref[...]).astype(os_ref.dtype)


def _matmul_with_side(x, w, w_side, tm, tn):
    m, k = x.shape
    n = w.shape[1]
    ns = w_side.shape[1]
    return pl.pallas_call(
        _mm_side_kernel,
        grid=(m // tm, n // tn),
        in_specs=[pl.BlockSpec((tm, k), lambda i, j: (i, 0)),
                  pl.BlockSpec((k, tn), lambda i, j: (0, j)),
                  pl.BlockSpec((k, ns), lambda i, j: (0, 0))],
        out_specs=[pl.BlockSpec((tm, tn), lambda i, j: (i, j)),
                   pl.BlockSpec((tm, ns), lambda i, j: (i, 0))],
        out_shape=[jax.ShapeDtypeStruct((m, n), F32), jax.ShapeDtypeStruct((m, ns), F32)],
        compiler_params=_cparams(("parallel", "arbitrary")),
        name="in_proj_n%d_side" % n,
    )(x, w, w_side)


def _out_proj_kernel(ya_ref, yb_ref, wa_ref, wb_ref, r_ref, o_ref):
    o_ref[...] = (r_ref[...] + _dot(ya_ref[...], wa_ref[...])
                  + _dot(yb_ref[...], wb_ref[...]))


def _out_proj(ya, yb, w, resid, tm, tn):
    m, ka = ya.shape
    kb = yb.shape[1]
    n = w.shape[1]
    assert ka == kb
    return pl.pallas_call(
        _out_proj_kernel,
        grid=(m // tm, n // tn),
        in_specs=[pl.BlockSpec((tm, ka), lambda i, j: (i, 0)),
                  pl.BlockSpec((tm, kb), lambda i, j: (i, 0)),
                  pl.BlockSpec((ka, tn), lambda i, j: (0, j)),
                  pl.BlockSpec((kb, tn), lambda i, j: (1, j)),
                  pl.BlockSpec((tm, tn), lambda i, j: (i, j))],
        out_specs=pl.BlockSpec((tm, tn), lambda i, j: (i, j)),
        out_shape=jax.ShapeDtypeStruct((m, n), F32),
        compiler_params=_cparams(("parallel", "parallel")),
        name="out_proj",
    )(ya, yb, w, w, resid)


def _sb_kernel(q_ref, k_ref, v_ref, gain_ref, o_ref,
               acc_ref, run_ref, lb_ref, hi_ref, lo_ref, rs_ref, *, tq, tk, scale):
    assert tq == 2 * tk
    i = pl.program_id(2)
    sub = HEAD_DIM
    nsub = tq // sub
    row = lax.broadcasted_iota(jnp.int32, (sub, tk), 0)
    col = lax.broadcasted_iota(jnp.int32, (sub, tk), 1)
    r2 = lax.broadcasted_iota(jnp.int32, (tk, tk), 0)
    c2 = lax.broadcasted_iota(jnp.int32, (tk, tk), 1)
    later = (r2 > c2).astype(BF16)
    acc_ref[...] = jnp.zeros_like(acc_ref)
    run_ref[...] = jnp.zeros_like(run_ref)

    def rows(r):
        return slice(r * sub, (r + 1) * sub)

    def mask_kind(diag_off, r):
        if diag_off is None or diag_off + tk - 1 < r * sub:
            return "all"
        if diag_off >= (r + 1) * sub - 1:
            return "none"
        return "part"

    def key_rows(j):
        return pl.ds(pl.multiple_of(j * tk, tk), tk)

    def matmuls(weigh_blk, score_blk, r):
        z = sfx = None
        if score_blk is not None and mask_kind(score_blk[2], r) != "none":
            z = _dot_nt(q_ref[0, rows(r), :], k_ref[0, key_rows(score_blk[0]), :])
        if weigh_blk is not None and mask_kind(weigh_blk[2], r) != "none":
            slot_w = weigh_blk[1]
            sfx = (_dot(hi_ref[slot_w, rows(r), :], later)
                   + _dot(lo_ref[slot_w, rows(r), :], later))
        return z, sfx

    def elementwise(weigh_blk, score_blk, r, z, sfx):
        if score_blk is not None and mask_kind(score_blk[2], r) != "none":
            _, slot_s, off_s = score_blk
            z = z * scale
            log_beta = jnp.minimum(z, 0.0) - jnp.log(1.0 + jnp.exp(-jnp.abs(z)))
            log_keep = log_beta - z
            if mask_kind(off_s, r) == "part":
                mask = (col + off_s) < (row + r * sub)
                log_keep = jnp.where(mask, log_keep, 0.0)
                log_beta = jnp.where(mask, log_beta, MASKED_LOG)
            hi, lo = _split(log_keep)
            lb_ref[slot_s, rows(r), :] = log_beta
            hi_ref[slot_s, rows(r), :] = hi
            lo_ref[slot_s, rows(r), :] = lo
            rs_ref[slot_s, rows(r), :] = jnp.broadcast_to(
                jnp.sum(log_keep, axis=-1, keepdims=True), (sub, HEAD_DIM))
        if weigh_blk is not None and mask_kind(weigh_blk[2], r) != "none":
            jw, slot_w, _ = weigh_blk
            run = jnp.concatenate([run_ref[rows(r), :]] * (tk // HEAD_DIM), axis=1)
            w = jnp.exp(lb_ref[slot_w, rows(r), :] + sfx + run)
            acc_ref[rows(r), :] += _dot(w.astype(BF16), v_ref[0, key_rows(jw), :])
            run_ref[rows(r), :] += rs_ref[slot_w, rows(r), :]

    def step(first, weigh_blk, score_blk, nxt):
        cur = matmuls(weigh_blk, score_blk, 0) if first is None else first
        for r in range(nsub):
            if r + 1 < nsub:
                ahead = matmuls(weigh_blk, score_blk, r + 1)
            else:
                ahead = matmuls(nxt[0], nxt[1], 0) if nxt is not None else None
            elementwise(weigh_blk, score_blk, r, *cur)
            cur = ahead
        return cur

    step(None, None, (2 * i + 1, 0, tk), None)
    first = step(None, (2 * i + 1, 0, tk), (2 * i, 1, 0),
                 ((2 * i, 1, None), (jnp.maximum(2 * i - 1, 0), 0, None)))

    def body(p, first):
        j = 2 * (i - p)
        first = step(first, (j, 1, None), (j - 1, 0, None),
                     ((j - 1, 0, None), (j - 2, 1, None)))
        return step(first, (j - 1, 0, None), (j - 2, 1, None),
                    ((j - 2, 1, None), (jnp.maximum(j - 3, 0), 0, None)))

    first = lax.fori_loop(0, i, body, first)
    step(first, (0, 1, None), None, None)
    o_ref[0] = _rms(acc_ref[...], gain_ref[...]).astype(o_ref.dtype)


def _sb_attention(proj_sb, gain, tq, tk):
    b, s, _ = proj_sb.shape
    nh = SB_HEADS
    return pl.pallas_call(
        functools.partial(_sb_kernel, tq=tq, tk=tk, scale=1.0 / math.sqrt(HEAD_DIM)),
        grid=(b, nh, s // tq),
        in_specs=[
            pl.BlockSpec((1, tq, HEAD_DIM), lambda bi, h, i: (bi, i, h)),
            pl.BlockSpec((1, s, HEAD_DIM), lambda bi, h, i: (bi, 0, nh + h)),
            pl.BlockSpec((1, s, HEAD_DIM), lambda bi, h, i: (bi, 0, 2 * nh + h)),
            pl.BlockSpec((1, HEAD_DIM), lambda bi, h, i: (0, 0)),
        ],
        out_specs=pl.BlockSpec((1, tq, HEAD_DIM), lambda bi, h, i: (bi, i, h)),
        out_shape=jax.ShapeDtypeStruct((b, s, D_SB), BF16),
        scratch_shapes=[pltpu.VMEM((tq, HEAD_DIM), F32), pltpu.VMEM((tq, HEAD_DIM), F32),
                        pltpu.VMEM((2, tq, tk), F32), pltpu.VMEM((2, tq, tk), BF16),
                        pltpu.VMEM((2, tq, tk), BF16), pltpu.VMEM((2, tq, HEAD_DIM), F32)],
        compiler_params=_cparams(("parallel", "parallel", "arbitrary")),
        name="sb_attn",
    )(proj_sb, proj_sb, proj_sb, gain)


def _dot3(a_hi, a_lo, b_hi, b_lo):
    return _dot(a_hi, b_hi) + (_dot(a_lo, b_hi) + _dot(a_hi, b_lo))


def _unit_lower_inverse(a_bds, same_chunk):
    n = GROUP // CHUNK
    r = lax.broadcasted_iota(jnp.int32, (CHUNK, GROUP), 0)
    cl = lax.broadcasted_iota(jnp.int32, (CHUNK, GROUP), 1) & (CHUNK - 1)
    same16 = (r >> 4) == (cl >> 4)
    same32 = (r >> 5) == (cl >> 5)
    eye_p = (r == cl).astype(F32)

    def packed(a_bd):
        a_p = a_bd[0:CHUNK]
        for c in range(1, n):
            a_p = a_p + a_bd[c * CHUNK:(c + 1) * CHUNK]
        return a_p

    def block_diag(x_p):
        return jnp.concatenate([x_p] * n, axis=0) * same_chunk().astype(x_p.dtype)

    def mm(xs, ys):
        out = []
        for x_p, y_p in zip(xs, ys):
            xh, xl = _split(x_p)
            yh, yl = _split(y_p)
            out.append(_dot3(xh, xl, block_diag(yh), block_diag(yl)))
        return out

    def mm1(xs, ys):
        return [_dot(x_p.astype(BF16), block_diag(y_p.astype(BF16))) for x_p, y_p in zip(xs, ys)]

    def axpy(xs, ys, sign):
        return [x + y if sign > 0 else x - y for x, y in zip(xs, ys)]

    a_ps = [packed(a) for a in a_bds]
    a16 = [jnp.where(same16, a, 0.0) for a in a_ps]
    x = [eye_p - a for a in a16]
    p = mm1(a16, a16)
    x = axpy(x, mm1(x, p), 1.0)
    p = mm1(p, p)
    x = axpy(x, mm1(x, p), 1.0)
    p = mm1(p, p)
    x = axpy(x, mm1(x, p), 1.0)
    e32 = [jnp.where(same32 & ~same16, a, 0.0) for a in a_ps]
    x = axpy(x, mm(x, mm(e32, x)), -1.0)
    e64 = [jnp.where(same32, 0.0, a) for a in a_ps]
    x = axpy(x, mm(x, mm(e64, x)), -1.0)
    return [block_diag(xi) for xi in x]


M_SAME, M_LOWER, M_UPPER, M_EYE, M_STRICT = range(5)


def _gdn_masks():
    shape = (GROUP, GROUP)
    r = lax.broadcasted_iota(jnp.int32, shape, 0)
    c = lax.broadcasted_iota(jnp.int32, shape, 1)
    same = (r >> 6) == (c >> 6)
    return [m.astype(F32) for m in
            (same, same & (c <= r), same & (r <= c), r == c, same & (c < r))]


def _gdn_groups(qs, ks, vs, a_rows, b_rows, neg_rates, dt_biases, states, mask_ref):
    n = GROUP // CHUNK
    ng = len(qs)
    steps = (ng // len(states)) * n
    shape = (GROUP, GROUP)

    def mask(m):
        return mask_ref[m]

    def rowsum(x):
        return jnp.sum(x, axis=1, keepdims=True)

    def colsum(x):
        return jnp.sum(x, axis=0, keepdims=True)

    decay, exp_gc, kend_scale, glast_rows, beta_cols = [], [], [], [], []
    for a_row, b_row, neg_rate, dt_bias in zip(a_rows, b_rows, neg_rates, dt_biases):
        sp_in = a_row + dt_bias
        g_row = neg_rate * (jnp.maximum(sp_in, 0.0) + jnp.log(1.0 + jnp.exp(-jnp.abs(sp_in))))
        g_b = jnp.broadcast_to(g_row, shape)
        gc_col = rowsum(g_b * mask(M_LOWER))
        glast_col = rowsum(g_b * mask(M_SAME))
        g_col = rowsum(g_b * mask(M_EYE))
        beta_cols.append(rowsum(jnp.broadcast_to(_sigmoid(b_row), shape) * mask(M_EYE)))
        gcol_b = jnp.broadcast_to(g_col, shape)
        gc_row = colsum(gcol_b * mask(M_UPPER))
        glast_rows.append(colsum(gcol_b * mask(M_SAME)))
        decay.append(jnp.exp(jnp.where(mask(M_LOWER) > 0.5, gc_col - gc_row, MASKED_LOG)))
        exp_gc.append(jnp.exp(gc_col))
        kend_scale.append(jnp.exp(glast_col - gc_col))

    kbs = [k * b for k, b in zip(ks, beta_cols)]
    grams = [_dot_nt(jnp.concatenate([kb, q], axis=0).astype(BF16), k.astype(BF16))
             for kb, q, k in zip(kbs, qs, ks)]
    a_bds = [g[:GROUP] * d * mask(M_STRICT) for g, d in zip(grams, decay)]
    aqk_bds = [(g[GROUP:] * d).astype(BF16) for g, d in zip(grams, decay)]

    tinvs = _unit_lower_inverse(a_bds, functools.partial(mask, M_SAME))
    wus = []
    for tinv, kb, v, eg, bc in zip(tinvs, kbs, vs, exp_gc, beta_cols):
        th, tl = _split(tinv)
        rh, rl = _split(jnp.concatenate([kb * eg, v * bc], axis=1))
        wus.append(_dot3(th, tl, rh, rl))

    lane_chunk = lax.broadcasted_iota(jnp.int32, (HEAD_DIM, GROUP), 1) >> 6
    lane = lax.broadcasted_iota(jnp.int32, (1, GROUP), 1)
    pns, decs = [], []
    for k, scale, wu, gl in zip(ks, kend_scale, wus, glast_rows):
        kend_t = (k * scale).T
        wu_b = wu.astype(BF16)
        for ci in range(n):
            pns.append(_dot(jnp.where(lane_chunk == ci, kend_t, 0.0).astype(BF16), wu_b))
            decs.append(jnp.exp(jnp.sum(jnp.where(lane == ci * CHUNK, gl, 0.0),
                                        axis=1, keepdims=True)))

    states = list(states)
    chunk_states = [None] * len(pns)
    for t in range(steps):
        for hh in range(len(states)):
            idx = hh * steps + t
            pn = pns[idx]
            chunk_states[idx] = states[hh].astype(BF16)
            states[hh] = ((decs[idx] * states[hh] + pn[:, HEAD_DIM:])
                          - _dot(pn[:, :HEAD_DIM].astype(BF16), chunk_states[idx]))

    outs = []
    for gi in range(ng):
        w4 = wus[gi][:, :HEAD_DIM]
        u4 = wus[gi][:, HEAD_DIM:]
        qdec4 = qs[gi] * exp_gc[gi]
        v_new, o_q = [], []
        for ci in range(n):
            lo = ci * CHUNK
            lhs = jnp.concatenate([w4[lo:lo + CHUNK], qdec4[lo:lo + CHUNK]], axis=0).astype(BF16)
            res = _dot(lhs, chunk_states[gi * n + ci])
            v_new.append(u4[lo:lo + CHUNK] - res[:CHUNK])
            o_q.append(res[CHUNK:])
        v_new = jnp.concatenate(v_new, axis=0).astype(BF16)
        outs.append(jnp.concatenate(o_q, axis=0) + _dot(aqk_bds[gi], v_new))
    return outs, states


def _gdn_kernel(alog_ref, dtb_ref, q_ref, k_ref, v_ref, gate_ref, a_ref, b_ref,
                cwq_ref, cwk_ref, cwv_ref, gain_ref, o_ref,
                ext_ref, state_ref, mask_ref, *, ts, hps):
    hp = pl.program_id(1)
    s = pl.program_id(2)
    ng = ts // GROUP
    pad = 8

    @pl.when(s == 0)
    def _():
        ext_ref[:, 0:pad, :] = jnp.zeros((3, pad, hps * HEAD_DIM), F32)
        state_ref[...] = jnp.zeros_like(state_ref)
        for m, mask in enumerate(_gdn_masks()):
            mask_ref[m] = mask

    def conv_silu(idx, x_ref, cw_ref):
        ext_ref[idx, pad:pad + ts, :] = x_ref[0]
        acc = None
        for tap in range(CONV_W):
            off = pad - (CONV_W - 1) + tap
            term = cw_ref[tap:tap + 1, :] * ext_ref[idx, off:off + ts, :]
            acc = term if acc is None else acc + term
        ext_ref[idx, 0:pad, :] = ext_ref[idx, ts:ts + pad, :]
        return _silu(acc)

    ones = jnp.ones((HEAD_DIM, HEAD_DIM), BF16)

    def l2n(x):
        hi, lo = _split(x * x)
        sumsq = _dot(hi, ones) + _dot(lo, ones)
        return x * lax.rsqrt(sumsq + EPS)

    def groups(x, hh, fn=None):
        out = []
        for gi in range(ng):
            blk = x[gi * GROUP:(gi + 1) * GROUP, hh * HEAD_DIM:(hh + 1) * HEAD_DIM]
            out.append(blk if fn is None else fn(blk))
        return out

    q_all = conv_silu(0, q_ref, cwq_ref)
    k_all = conv_silu(1, k_ref, cwk_ref)
    v_all = conv_silu(2, v_ref, cwv_ref)
    qs, ks, vs, a_rows, b_rows, neg_rates, dt_biases = [], [], [], [], [], [], []
    for hh in range(hps):
        qs += groups(q_all, hh, lambda x: l2n(x) * (1.0 / math.sqrt(HEAD_DIM)))
        ks += groups(k_all, hh, l2n)
        vs += groups(v_all, hh)
        a_rows += [a_ref[0, hh, pl.ds(s * ng + gi, 1), :] for gi in range(ng)]
        b_rows += [b_ref[0, hh, pl.ds(s * ng + gi, 1), :] for gi in range(ng)]
        neg_rates += [-jnp.exp(jnp.full((1, GROUP), alog_ref[hp * hps + hh], F32))] * ng
        dt_biases += [jnp.full((1, GROUP), dtb_ref[hp * hps + hh], F32)] * ng

    outs, states = _gdn_groups(qs, ks, vs, a_rows, b_rows, neg_rates, dt_biases,
                               [state_ref[hh] for hh in range(hps)], mask_ref)
    gain = gain_ref[...]
    for hh in range(hps):
        state_ref[hh] = states[hh]
        lanes = slice(hh * HEAD_DIM, (hh + 1) * HEAD_DIM)
        for gi in range(ng):
            lo = gi * GROUP
            gate = gate_ref[0, lo:lo + GROUP, lanes]
            o_ref[0, lo:lo + GROUP, lanes] = (_rms(outs[hh * ng + gi], gain)
                                              * _silu(gate)).astype(o_ref.dtype)


def _gdn(proj_g, a_t, b_t, conv_w, a_log, dt_bias, gain, ts, hps):
    b, s, _ = proj_g.shape
    nhp = GDN_HEADS // hps
    width = hps * HEAD_DIM
    smem = pl.BlockSpec(memory_space=pltpu.SMEM)

    def col(group):
        return pl.BlockSpec((1, ts, width), lambda bi, h, si: (bi, si, group * nhp + h))

    def cw(group):
        return pl.BlockSpec((CONV_W, width), lambda bi, h, si: (0, group * nhp + h))

    ab = pl.BlockSpec((1, hps, s // GROUP, GROUP), lambda bi, h, si: (bi, h, 0, 0))
    return pl.pallas_call(
        functools.partial(_gdn_kernel, ts=ts, hps=hps),
        grid=(b, nhp, s // ts),
        in_specs=[smem, smem, col(0), col(1), col(2), col(3), ab, ab,
                  cw(0), cw(1), cw(2),
                  pl.BlockSpec((1, HEAD_DIM), lambda bi, h, si: (0, 0))],
        out_specs=pl.BlockSpec((1, ts, width), lambda bi, h, si: (bi, si, h)),
        out_shape=jax.ShapeDtypeStruct((b, s, D_GDN), BF16),
        scratch_shapes=[pltpu.VMEM((3, ts + 8, width), F32),
                        pltpu.VMEM((hps, HEAD_DIM, HEAD_DIM), F32),
                        pltpu.VMEM((5, GROUP, GROUP), F32)],
        compiler_params=_cparams(("parallel", "parallel", "arbitrary")),
        name="gdn",
    )(a_log, dt_bias, proj_g, proj_g, proj_g, proj_g, a_t, b_t,
      conv_w, conv_w, conv_w, gain)


def _pick(n, pref):
    t = min(pref, n)
    while n % t:
        t -= 128
    return t


def _ffn_weights(wg, wu, wd):
    return wg.astype(BF16), wu.astype(BF16), wd.astype(BF16)


def kernel(x, ffn1_norm, ffn1_w_gate, ffn1_w_up, ffn1_w_down, mix_norm, w_in, sb_out_norm, conv_w, a_log, dt_bias, gdn_out_norm, w_out, ffn2_norm, ffn2_w_gate, ffn2_w_up, ffn2_w_down, final_norm):
    bsz, s_len, d = x.shape
    t = bsz * s_len
    assert ffn1_norm.shape[0] == 1, "single-layer block"
    l = 0
    tm = _pick(t, 512)
    tf = 512
    x2d = x.reshape(t, d)
    row = lambda v: v.reshape(1, -1)

    wg, wu, wd = _ffn_weights(ffn1_w_gate[l], ffn1_w_up[l], ffn1_w_down[l])
    x1, hmix = _ffn(x2d, row(ffn1_norm[l]), wg, wu, wd, row(mix_norm[l]), "mix", tm, tf)

    w = w_in[l]
    o_qkv = 3 * D_SB
    o_a = o_qkv + 3 * D_GDN
    o_gate = o_a + 2 * GDN_HEADS
    w_sb = w[:, :o_qkv].astype(BF16)
    w_g = jnp.concatenate([w[:, o_qkv:o_a], w[:, o_gate:]], axis=1).astype(BF16)
    w_ab = jnp.pad(w[:, o_a:o_gate], ((0, 0), (0, HEAD_DIM - 2 * GDN_HEADS))).astype(BF16)
    tmm = _pick(t, 1024)
    proj_sb = _matmul(hmix, w_sb, BF16, tmm, _pick(3 * D_SB, 1536)).reshape(bsz, s_len, 3 * D_SB)
    proj_g, proj_ab = _matmul_with_side(hmix, w_g, w_ab, tmm, _pick(4 * D_GDN, 2048))
    proj_g = proj_g.reshape(bsz, s_len, 4 * D_GDN)
    proj_ab = proj_ab.reshape(bsz, s_len, HEAD_DIM)

    def per_head(cols):
        return cols.transpose(0, 2, 1).reshape(bsz, GDN_HEADS, s_len // GROUP, GROUP)

    a_t = per_head(proj_ab[..., :GDN_HEADS])
    b_t = per_head(proj_ab[..., GDN_HEADS:2 * GDN_HEADS])

    tq = _pick(s_len, 512)
    o_sb = _sb_attention(proj_sb, row(sb_out_norm[l]), tq, _pick(tq, 256))
    o_g = _gdn(proj_g, a_t, b_t, conv_w[l], a_log[l], dt_bias[l],
               row(gdn_out_norm[l]), _pick(s_len, 512), 4)

    x2 = _out_proj(o_sb.reshape(t, D_SB), o_g.reshape(t, D_GDN),
                   w_out[l].astype(BF16), x1, tmm, _pick(d, 1024))

    wg, wu, wd = _ffn_weights(ffn2_w_gate[l], ffn2_w_up[l], ffn2_w_down[l])
    (out,) = _ffn(x2, row(ffn2_norm[l]), wg, wu, wd, row(final_norm), "final", tm, tf)
    return out.reshape(bsz, s_len, d)
```
